```python
import jax, jax.numpy as jnp
from jax import lax
import numpy as np

D_MODEL = 1024
BATCH = 2
SEQ = 16384
DEPTH = 1
DEC_BATCH = 8
DEC_SEQ = 2048
PAST_LEN = 128

GRID_W = 64
MLA_HEADS = 8
Q_LORA = 256
KV_LORA = 128
QK_NOPE = 64
QK_ROPE = 32
V_DIM = 64
ROPE_THETA = 10000.0
Q_BLOCK = 128
NA_HEADS = 8
NA_DIM = 64
NA_WIN_R = 8
NA_WIN_C = 16
NA_COL_BLK = 16
NA_KEY_COLS = NA_COL_BLK + NA_WIN_C
MLA_WIDTH = MLA_HEADS * V_DIM
NA_WIDTH = NA_HEADS * NA_DIM
MIX_WIDTH = MLA_WIDTH + NA_WIDTH
IN_WIDTH = Q_LORA + KV_LORA + QK_ROPE + 3 * NA_WIDTH
D_FF = 2816
CONV_W = 3
EPS = 1e-6
NEG_INF = -1e30

kernel_name = "hybrid_mla_natten_convffn_encoder"


def rms_norm(x, g):
    xf = x.astype(jnp.float32)
    y = xf * lax.rsqrt(jnp.mean(xf * xf, axis=-1, keepdims=True) + EPS)
    return (y * g.astype(jnp.float32)).astype(x.dtype)


def rope_tables(S):
    inv = 1.0 / (ROPE_THETA ** (jnp.arange(0, QK_ROPE, 2, dtype=jnp.float32) / QK_ROPE))
    ang = jnp.arange(S, dtype=jnp.float32)[:, None] * inv[None, :]
    return jnp.cos(ang), jnp.sin(ang)


def apply_rope(x, cos, sin):
    x1, x2 = jnp.split(x.astype(jnp.float32), 2, axis=-1)
    return jnp.concatenate([x1 * cos - x2 * sin, x1 * sin + x2 * cos], axis=-1).astype(x.dtype)


def mla_attention(c_q, c_kv, k_rope, g_q_lat, w_q_up, g_kv_lat, w_kv_up):
    B, S, _ = c_q.shape
    cos, sin = rope_tables(S)
    q = (rms_norm(c_q, g_q_lat) @ w_q_up).reshape(B, S, MLA_HEADS, QK_NOPE + QK_ROPE)
    q = jnp.concatenate([q[..., :QK_NOPE], apply_rope(q[..., QK_NOPE:], cos[:, None, :], sin[:, None, :])], axis=-1)
    kv = (rms_norm(c_kv, g_kv_lat) @ w_kv_up).reshape(B, S, MLA_HEADS, QK_NOPE + V_DIM)
    k_pe = apply_rope(k_rope, cos, sin)
    k = jnp.concatenate([kv[..., :QK_NOPE], jnp.broadcast_to(k_pe[:, :, None, :], (B, S, MLA_HEADS, QK_ROPE))], axis=-1)
    v = kv[..., QK_NOPE:]
    scale = (QK_NOPE + QK_ROPE) ** -0.5
    nb = S // Q_BLOCK
    qb = q.reshape(B, nb, Q_BLOCK, MLA_HEADS, QK_NOPE + QK_ROPE).transpose(1, 0, 2, 3, 4)

    def block(qi):
        s = jnp.einsum('bqhd,bkhd->bhqk', qi, k, preferred_element_type=jnp.float32) * scale
        p = jax.nn.softmax(s, axis=-1)
        return jnp.einsum('bhqk,bkhd->bqhd', p.astype(v.dtype), v)

    o = lax.map(block, qb)
    return o.transpose(1, 0, 2, 3, 4).reshape(B, S, MLA_WIDTH)


def neighborhood_attention(q, k, v, rpb):
    B, S, _ = q.shape
    rows = S // GRID_W
    kr = min(NA_WIN_R, rows)
    ncb = GRID_W // NA_COL_BLK
    shp = (B, rows, GRID_W, NA_HEADS, NA_DIM)
    q, k, v = q.reshape(shp), k.reshape(shp), v.reshape(shp)
    r = jnp.arange(rows, dtype=jnp.int32)
    row_start = jnp.clip(r - kr // 2, 0, rows - kr)
    key_rows = row_start[:, None] + jnp.arange(kr, dtype=jnp.int32)[None, :]
    j = jnp.arange(ncb, dtype=jnp.int32)
    blk_col_start = jnp.clip(j * NA_COL_BLK - NA_WIN_C // 2, 0, GRID_W - NA_KEY_COLS)
    key_cols = blk_col_start[:, None] + jnp.arange(NA_KEY_COLS, dtype=jnp.int32)[None, :]
    kb = k[:, key_rows[:, None, :, None], key_cols[None, :, None, :]]
    vb = v[:, key_rows[:, None, :, None], key_cols[None, :, None, :]]
    qb = q.reshape(B, rows, ncb, NA_COL_BLK, NA_HEADS, NA_DIM)
    s = jnp.einsum('brjqhd,brjikhd->bhrjqik', qb, kb, preferred_element_type=jnp.float32) * (NA_DIM ** -0.5)
    qcol = j[:, None] * NA_COL_BLK + jnp.arange(NA_COL_BLK, dtype=jnp.int32)[None, :]
    qcol_start = jnp.clip(qcol - NA_WIN_C // 2, 0, GRID_W - NA_WIN_C)
    kc = key_cols[:, None, :]
    col_valid = (kc >= qcol_start[:, :, None]) & (kc < qcol_start[:, :, None] + NA_WIN_C)
    dr_idx = key_rows - r[:, None] + (NA_WIN_R - 1)
    dc_idx = jnp.clip(kc - qcol[:, :, None] + (NA_WIN_C - 1), 0, 2 * NA_WIN_C - 2)
    bias = rpb.astype(jnp.float32)[:, dr_idx[:, None, None, :, None], dc_idx[None, :, :, None, :]]
    s = jnp.where(col_valid[:, :, None, :], s + bias[None], NEG_INF)
    p = jax.nn.softmax(s.reshape(B, NA_HEADS, rows, ncb, NA_COL_BLK, kr * NA_KEY_COLS), axis=-1)
    o = jnp.einsum('bhrjqn,brjnhd->brjqhd', p.astype(vb.dtype),
                   vb.reshape(B, rows, ncb, kr * NA_KEY_COLS, NA_HEADS, NA_DIM))
    return o.reshape(B, S, NA_WIDTH)


def conv_ffn(x, w_up, conv_w, conv_b, w_down):
    S = x.shape[1]
    h = x @ w_up
    pad = CONV_W // 2
    hp = jnp.pad(h, ((0, 0), (pad, pad), (0, 0)))
    hc = conv_b
    for t in range(CONV_W):
        hc = hc + hp[:, t:t + S] * conv_w[t]
    g, u = jnp.split(hc, 2, axis=-1)
    return (jax.nn.gelu(g, approximate=True) * u) @ w_down


def encoder_layer(x, g_mix_pre, w_in, g_q_lat, w_q_up, g_kv_lat, w_kv_up, na_rpb, w_o, g_mix_post,
                  g_ffn_pre, w_ffn_up, ffn_conv_w, ffn_conv_b, w_ffn_down, g_ffn_post):
    h = rms_norm(x, g_mix_pre)
    z = h @ w_in
    o1 = Q_LORA
    o2 = o1 + KV_LORA
    o3 = o2 + QK_ROPE
    c_q, c_kv, k_rope = z[..., :o1], z[..., o1:o2], z[..., o2:o3]
    nq = z[..., o3:o3 + NA_WIDTH]
    nk = z[..., o3 + NA_WIDTH:o3 + 2 * NA_WIDTH]
    nv = z[..., o3 + 2 * NA_WIDTH:]
    a = mla_attention(c_q, c_kv, k_rope, g_q_lat, w_q_up, g_kv_lat, w_kv_up)
    n = neighborhood_attention(nq, nk, nv, na_rpb)
    mix = jnp.concatenate([a, n], axis=-1) @ w_o
    x = x + rms_norm(mix, g_mix_post)
    h = rms_norm(x, g_ffn_pre)
    x = x + rms_norm(conv_ffn(h, w_ffn_up, ffn_conv_w, ffn_conv_b, w_ffn_down), g_ffn_post)
    return x


def setup_inputs(seed: int = 0) -> dict:
    key = jax.random.key(seed)
    ks = jax.random.split(key, 20)
    f32 = jnp.float32

    def nrm(k, shape, scale):
        return jax.random.normal(k, shape, f32) * scale

    def gain(k, n):
        return 1.0 + 0.02 * jax.random.normal(k, (DEPTH, n), f32)

    return {
        "x_prompt": jax.random.normal(ks[0], (BATCH, SEQ, D_MODEL), f32),
        "x_sample": jax.random.normal(ks[1], (DEC_BATCH, DEC_SEQ, D_MODEL), f32),
        "g_mix_pre": gain(ks[2], D_MODEL),
        "w_in": nrm(ks[3], (DEPTH, D_MODEL, IN_WIDTH), D_MODEL ** -0.5),
        "g_q_lat": gain(ks[4], Q_LORA),
        "w_q_up": nrm(ks[5], (DEPTH, Q_LORA, MLA_HEADS * (QK_NOPE + QK_ROPE)), Q_LORA ** -0.5),
        "g_kv_lat": gain(ks[6], KV_LORA),
        "w_kv_up": nrm(ks[7], (DEPTH, KV_LORA, MLA_HEADS * (QK_NOPE + V_DIM)), KV_LORA ** -0.5),
        "na_rpb": nrm(ks[8], (DEPTH, NA_HEADS, 2 * NA_WIN_R - 1, 2 * NA_WIN_C - 1), 0.1),
        "w_o": nrm(ks[9], (DEPTH, MIX_WIDTH, D_MODEL), MIX_WIDTH ** -0.5),
        "g_mix_post": gain(ks[10], D_MODEL),
        "g_ffn_pre": gain(ks[11], D_MODEL),
        "w_ffn_up": nrm(ks[12], (DEPTH, D_MODEL, 2 * D_FF), D_MODEL ** -0.5),
        "ffn_conv_w": nrm(ks[13], (DEPTH, CONV_W, 2 * D_FF), CONV_W ** -0.5),
        "ffn_conv_b": nrm(ks[14], (DEPTH, 2 * D_FF), 0.01),
        "w_ffn_down": nrm(ks[15], (DEPTH, D_FF, D_MODEL), D_FF ** -0.5),
        "g_ffn_post": gain(ks[16], D_MODEL),
    }


def reference(x_prompt, x_sample, g_mix_pre, w_in, g_q_lat, w_q_up, g_kv_lat, w_kv_up, na_rpb, w_o,
              g_mix_post, g_ffn_pre, w_ffn_up, ffn_conv_w, ffn_conv_b, w_ffn_down, g_ffn_post):
    y_prompt = x_prompt
    y_sample = x_sample
    for l in range(DEPTH):
        params = (g_mix_pre[l], w_in[l], g_q_lat[l], w_q_up[l], g_kv_lat[l], w_kv_up[l], na_rpb[l], w_o[l],
                  g_mix_post[l], g_ffn_pre[l], w_ffn_up[l], ffn_conv_w[l], ffn_conv_b[l], w_ffn_down[l],
                  g_ffn_post[l])
        y_prompt = encoder_layer(y_prompt, *params)
        y_sample = encoder_layer(y_sample, *params)
    return (y_prompt, y_sample)
```

```python
import functools

import jax
import jax.numpy as jnp
from jax import lax
from jax.experimental import pallas as pl
from jax.experimental.pallas import tpu as pltpu

D_MODEL = 1024
GRID_W = 64
MLA_HEADS = 8
Q_LORA = 256
KV_LORA = 128
QK_NOPE = 64
QK_ROPE = 32
V_DIM = 64
ROPE_THETA = 10000.0
NA_HEADS = 8
NA_DIM = 64
NA_WIN_R = 8
NA_WIN_C = 16
NA_WIDTH = NA_HEADS * NA_DIM
MLA_WIDTH = MLA_HEADS * V_DIM
D_FF = 2816
EPS = 1e-6
NEG_INF = -1e30

LANES = 128
HALO = 8
VMEM_LIMIT = 56 * 1024 * 1024

BF16 = jnp.bfloat16
F32 = jnp.float32


def _rms(x, g):
    return x * lax.rsqrt(jnp.mean(x * x, axis=-1, keepdims=True) + EPS) * g


def _dot(a, b):
    return jnp.dot(a, b, preferred_element_type=F32)


def _dot_nt(a, b):
    return lax.dot_general(a, b, (((1,), (1,)), ((), ())), preferred_element_type=F32)


def _const_spec(shape):
    return pl.BlockSpec(shape, lambda *_: (0,) * len(shape))


def _proj_kernel(x_ref, g_pre_ref, w_lat_ref, w_nq_ref, w_nk_ref, w_nv_ref, g_q_ref, w_q_ref, w_qrot_ref,
                 g_kv_ref, w_k_ref, w_v_ref, cos_ref, sin_ref,
                 q_ref, k_ref, v_ref, nq_ref, nk_ref, nv_ref):
    h = _rms(x_ref[0], g_pre_ref[...]).astype(BF16)
    cos = cos_ref[...]
    sin = sin_ref[...]

    z = _dot(h, w_lat_ref[...])
    c_q = z[:, :Q_LORA]
    c_kv = z[:, Q_LORA:Q_LORA + KV_LORA]
    k_r = z[:, Q_LORA + KV_LORA:Q_LORA + KV_LORA + LANES]
    k_rrot = z[:, Q_LORA + KV_LORA + LANES:]
    k_pe = k_r * cos + k_rrot * sin

    cqn = _rms(c_q, g_q_ref[...]).astype(BF16)
    qm = _dot(cqn, w_q_ref[...])
    qr = _dot(cqn, w_qrot_ref[...])
    ckvn = _rms(c_kv, g_kv_ref[...]).astype(BF16)
    kn = _dot(ckvn, w_k_ref[...])
    vv = _dot(ckvn, w_v_ref[...])
    for hd in range(MLA_HEADS):
        sl = slice(hd * LANES, (hd + 1) * LANES)
        q_ref[0, hd] = (qm[:, sl] * cos + qr[:, sl] * sin).astype(BF16)
        k_ref[0, hd] = (kn[:, sl] + k_pe).astype(BF16)
    for p in range(MLA_HEADS // 2):
        v_ref[0, p] = vv[:, p * LANES:(p + 1) * LANES].astype(BF16)

    nq = _dot(h, w_nq_ref[...]) * (NA_DIM ** -0.5)
    nk = _dot(h, w_nk_ref[...])
    nv = _dot(h, w_nv_ref[...])
    for p in range(NA_HEADS // 2):
        sl = slice(p * LANES, (p + 1) * LANES)
        nq_ref[0, p] = nq[:, sl].astype(BF16)
        nk_ref[0, p] = nk[:, sl].astype(BF16)
        nv_ref[0, p] = nv[:, sl].astype(BF16)


def _proj(x, wts, cos_tab, sin_tab, tm):
    B, S, _ = x.shape
    nt = S // tm
    hp = MLA_HEADS // 2
    head_out = lambda n: pl.BlockSpec((1, n, tm, LANES), lambda b, i: (b, 0, i, 0))
    names = ("g_mix_pre", "w_lat", "w_nq", "w_nk", "w_nv", "g_q_lat", "w_q", "w_qrot", "g_kv_lat", "w_k", "w_v")
    consts = [wts[n] for n in names]
    return pl.pallas_call(
        _proj_kernel,
        grid=(B, nt),
        in_specs=[pl.BlockSpec((1, tm, D_MODEL), lambda b, i: (b, i, 0))]
        + [_const_spec(c.shape) for c in consts]
        + [pl.BlockSpec((tm, LANES), lambda b, i: (i, 0))] * 2,
        out_specs=[head_out(MLA_HEADS), head_out(MLA_HEADS), head_out(hp), head_out(hp), head_out(hp), head_out(hp)],
        out_shape=[jax.ShapeDtypeStruct((B, MLA_HEADS, S, LANES), BF16)] * 2
        + [jax.ShapeDtypeStruct((B, hp, S, LANES), BF16)] * 4,
        compiler_params=pltpu.CompilerParams(
            dimension_semantics=("parallel", "parallel"), vmem_limit_bytes=VMEM_LIMIT),
        name="proj",
    )(x, *consts, cos_tab, sin_tab)


def _mla_kernel(q_ref, k_ref, v_ref, o_ref, *, tk, scale):
    tq = q_ref.shape[2]
    S = k_ref.shape[2]
    q0 = q_ref[0, 0]
    q1 = q_ref[0, 1]

    def step(c, carry):
        ks = pl.ds(pl.multiple_of(c * tk, tk), tk)
        v = v_ref[0, 0, ks, :]
        out = []
        for hd, q in ((0, q0), (1, q1)):
            m, l, acc = carry[hd]
            s = _dot_nt(q, k_ref[0, hd, ks, :]) * scale
            m_new = jnp.maximum(m, jnp.max(s, axis=-1, keepdims=True))
            alpha = jnp.exp(m - m_new)
            p = jnp.exp(s - m_new)
            l = alpha * l + jnp.sum(p, axis=-1, keepdims=True)
            acc = alpha * acc + _dot(p.astype(BF16), v)
            out.append((m_new, l, acc))
        return tuple(out)

    init = tuple((jnp.full((tq, 1), -jnp.inf, F32), jnp.zeros((tq, 1), F32), jnp.zeros((tq, LANES), F32))
                 for _ in range(2))
    (m0, l0, a0), (m1, l1, a1) = lax.fori_loop(0, S // tk, step, init)
    lane = lax.broadcasted_iota(jnp.int32, (tq, LANES), 1)
    o_ref[0] = jnp.where(lane < V_DIM, a0 / l0, a1 / l1).astype(o_ref.dtype)


def _mla(q, k, v, tq, tk):
    B, H, S, _ = q.shape
    kern = functools.partial(_mla_kernel, tk=tk, scale=(QK_NOPE + QK_ROPE) ** -0.5)
    return pl.pallas_call(
        kern,
        grid=(B, H // 2, S // tq),
        in_specs=[pl.BlockSpec((1, 2, tq, LANES), lambda b, p, i: (b, p, i, 0)),
                  pl.BlockSpec((1, 2, S, LANES), lambda b, p, i: (b, p, 0, 0)),
                  pl.BlockSpec((1, 1, S, LANES), lambda b, p, i: (b, p, 0, 0))],
        out_specs=pl.BlockSpec((1, tq, LANES), lambda b, p, i: (b, i, p)),
        out_shape=jax.ShapeDtypeStruct((B, S, MLA_WIDTH), BF16),
        compiler_params=pltpu.CompilerParams(
            dimension_semantics=("parallel", "parallel", "arbitrary"), vmem_limit_bytes=VMEM_LIMIT),
        name="mla_flash",
    )(q, k, v)


def _na_kernel(q_ref, k_ref, v_ref, bias_ref, mask_ref, o_ref, *, rows, rt):
    band = NA_WIN_R * GRID_W
    i = pl.program_id(2)
    lane = lax.broadcasted_iota(jnp.int32, (GRID_W, LANES), 1)
    low = lane < NA_DIM
    valid = mask_ref[...] > 0.0

    def row(j, carry):
        r = i * rt + j
        rs = jnp.clip(r - NA_WIN_R // 2, 0, rows - NA_WIN_R)
        off = rs - r + (NA_WIN_R - 1)
        qp = q_ref[0, 0, pl.ds(pl.multiple_of(j * GRID_W, GRID_W), GRID_W), :]
        ks = pl.ds(pl.multiple_of(rs * GRID_W, GRID_W), band)
        kb = k_ref[0, 0, ks, :]
        vb = v_ref[0, 0, ks, :]
        outs = []
        for hd in range(2):
            qh = jnp.where(low if hd == 0 else ~low, qp, jnp.zeros_like(qp))
            s = _dot_nt(qh, kb)
            s = jnp.where(valid, s + bias_ref[hd, off], NEG_INF)
            m = jnp.max(s, axis=-1, keepdims=True)
            p = jnp.exp(s - m)
            l = jnp.sum(p, axis=-1, keepdims=True)
            outs.append(_dot(p.astype(BF16), vb) / l)
        o_ref[0, pl.ds(pl.multiple_of(j * GRID_W, GRID_W), GRID_W), :] = (
            jnp.where(low, outs[0], outs[1]).astype(o_ref.dtype))
        return carry

    lax.fori_loop(0, rt, row, 0)


def _natten(q, k, v, bias_tab, mask, rt):
    B, HP, S, _ = q.shape
    rows = S // GRID_W
    assert rows >= NA_WIN_R and rows % rt == 0
    band = NA_WIN_R * GRID_W
    kern = functools.partial(_na_kernel, rows=rows, rt=rt)
    return pl.pallas_call(
        kern,
        grid=(B, HP, rows // rt),
        in_specs=[pl.BlockSpec((1, 1, rt * GRID_W, LANES), lambda b, p, i: (b, p, i, 0)),
                  pl.BlockSpec((1, 1, S, LANES), lambda b, p, i: (b, p, 0, 0)),
                  pl.BlockSpec((1, 1, S, LANES), lambda b, p, i: (b, p, 0, 0)),
                  pl.BlockSpec((2, NA_WIN_R, GRID_W, band), lambda b, p, i: (p, 0, 0, 0)),
                  _const_spec(mask.shape)],
        out_specs=pl.BlockSpec((1, rt * GRID_W, LANES), lambda b, p, i: (b, i, p)),
        out_shape=jax.ShapeDtypeStruct((B, S, NA_WIDTH), BF16),
        compiler_params=pltpu.CompilerParams(
            dimension_semantics=("parallel", "parallel", "arbitrary"), vmem_limit_bytes=VMEM_LIMIT),
        name="natten",
    )(q, k, v, bias_tab, mask)


def _oproj_kernel(a_ref, n_ref, x_ref, wa_ref, wn_ref, g_ref, o_ref):
    mix = _dot(a_ref[0], wa_ref[...]) + _dot(n_ref[0], wn_ref[...])
    o_ref[0] = x_ref[0] + _rms(mix, g_ref[...])


def _oproj(a, n, x, wts, tm):
    B, S, _ = x.shape
    tok = lambda w: pl.BlockSpec((1, tm, w), lambda b, i: (b, i, 0))
    consts = [wts["w_o_a"], wts["w_o_n"], wts["g_mix_post"]]
    return pl.pallas_call(
        _oproj_kernel,
        grid=(B, S // tm),
        in_specs=[tok(MLA_WIDTH), tok(NA_WIDTH), tok(D_MODEL)] + [_const_spec(c.shape) for c in consts],
        out_specs=tok(D_MODEL),
        out_shape=jax.ShapeDtypeStruct((B, S, D_MODEL), F32),
        compiler_params=pltpu.CompilerParams(
            dimension_semantics=("parallel", "parallel"), vmem_limit_bytes=VMEM_LIMIT),
        name="out_proj",
    )(a, n, x, *consts)


def _gelu_tanh(x):
    return 0.5 * x * (1.0 + jnp.tanh(0.7978845608028654 * (x + 0.044715 * (x * x * x))))


def _ffn_kernel(x_ref, prev_ref, next_ref, g_pre_ref, w_up_ref, cw_ref, cb_ref, w_dn_ref, g_post_ref, o_ref, *, fc):
    i = pl.program_id(1)
    last = pl.num_programs(1) - 1
    x = x_ref[0]
    tm = x.shape[0]
    prev = jnp.where(i > 0, prev_ref[0], 0.0)
    nxt = jnp.where(i < last, next_ref[0], 0.0)
    xh = jnp.concatenate([prev, x, nxt], axis=0)
    hn = _rms(xh, g_pre_ref[...]).astype(BF16)
    n_ext = tm + 2 * HALO

    def conv(hx, col):
        w = cw_ref[:, col:col + fc]
        left = pltpu.roll(hx, 1, 0)[HALO:HALO + tm]
        right = pltpu.roll(hx, n_ext - 1, 0)[HALO:HALO + tm]
        return cb_ref[:, col:col + fc] + left * w[0:1] + hx[HALO:HALO + tm] * w[1:2] + right * w[2:3]

    acc = jnp.zeros((tm, D_MODEL), F32)
    for c in range(D_FF // fc):
        cg = c * fc
        cu = D_FF + c * fc
        g = conv(_dot(hn, w_up_ref[:, cg:cg + fc]), cg)
        u = conv(_dot(hn, w_up_ref[:, cu:cu + fc]), cu)
        act = (_gelu_tanh(g) * u).astype(BF16)
        acc = acc + _dot(act, w_dn_ref[cg:cg + fc, :])
    o_ref[0] = x + _rms(acc, g_post_ref[...])


def _ffn(x, wts, tm, fc):
    B, S, _ = x.shape
    nb = tm // HALO
    kern = functools.partial(_ffn_kernel, fc=fc)
    consts = [wts["g_ffn_pre"], wts["w_ffn_up"], wts["ffn_conv_w"], wts["ffn_conv_b"], wts["w_ffn_down"],
              wts["g_ffn_post"]]
    return pl.pallas_call(
        kern,
        grid=(B, S // tm),
        in_specs=[pl.BlockSpec((1, tm, D_MODEL), lambda b, i: (b, i, 0)),
                  pl.BlockSpec((1, HALO, D_MODEL), lambda b, i: (b, jnp.maximum(i * nb - 1, 0), 0)),
                  pl.BlockSpec((1, HALO, D_MODEL), lambda b, i: (b, jnp.minimum((i + 1) * nb, S // HALO - 1), 0))]
        + [pl.BlockSpec(c.shape, lambda b, i, nd=c.ndim: (0,) * nd, pipeline_mode=pl.Buffered(1)) for c in consts],
        out_specs=pl.BlockSpec((1, tm, D_MODEL), lambda b, i: (b, i, 0)),
        out_shape=jax.ShapeDtypeStruct((B, S, D_MODEL), F32),
        compiler_params=pltpu.CompilerParams(
            dimension_semantics=("parallel", "parallel"), vmem_limit_bytes=VMEM_LIMIT),
        name="conv_ffn",
    )(x, x, x, *consts)


def _prep_weights(g_mix_pre, w_in, g_q_lat, w_q_up, g_kv_lat, w_kv_up, w_o, g_mix_post,
                  g_ffn_pre, w_ffn_up, ffn_conv_w, ffn_conv_b, w_ffn_down, g_ffn_post):
    half = QK_ROPE // 2
    o1, o2, o3 = Q_LORA, Q_LORA + KV_LORA, Q_LORA + KV_LORA + QK_ROPE
    row = lambda g: g.reshape(1, -1).astype(F32)

    w_kr = w_in[:, o2:o3]
    zl = jnp.zeros((D_MODEL, QK_NOPE), F32)
    zr = jnp.zeros((D_MODEL, LANES - QK_NOPE - QK_ROPE), F32)
    w_kr_blk = jnp.concatenate([zl, w_kr, zr], axis=1)
    w_krrot_blk = jnp.concatenate([zl, -w_kr[:, half:], w_kr[:, :half], zr], axis=1)
    w_lat = jnp.concatenate([w_in[:, :o2], w_kr_blk, w_krrot_blk], axis=1)

    wq = w_q_up.reshape(Q_LORA, MLA_HEADS, QK_NOPE + QK_ROPE)
    zq = jnp.zeros((Q_LORA, MLA_HEADS, LANES - QK_NOPE - QK_ROPE), F32)
    w_q = jnp.concatenate([wq, zq], axis=2).reshape(Q_LORA, MLA_HEADS * LANES)
    w_qrot = jnp.concatenate([jnp.zeros((Q_LORA, MLA_HEADS, QK_NOPE), F32), -wq[:, :, QK_NOPE + half:],
                              wq[:, :, QK_NOPE:QK_NOPE + half], zq], axis=2).reshape(Q_LORA, MLA_HEADS * LANES)

    wkv = w_kv_up.reshape(KV_LORA, MLA_HEADS, QK_NOPE + V_DIM)
    w_k = jnp.concatenate([wkv[:, :, :QK_NOPE], jnp.zeros((KV_LORA, MLA_HEADS, LANES - QK_NOPE), F32)],
                          axis=2).reshape(KV_LORA, MLA_HEADS * LANES)
    w_v = wkv[:, :, QK_NOPE:].reshape(KV_LORA, MLA_HEADS * V_DIM)

    return {
        "g_mix_pre": row(g_mix_pre), "w_lat": w_lat.astype(BF16),
        "w_nq": w_in[:, o3:o3 + NA_WIDTH].astype(BF16),
        "w_nk": w_in[:, o3 + NA_WIDTH:o3 + 2 * NA_WIDTH].astype(BF16),
        "w_nv": w_in[:, o3 + 2 * NA_WIDTH:].astype(BF16),
        "g_q_lat": row(g_q_lat), "w_q": w_q.astype(BF16), "w_qrot": w_qrot.astype(BF16),
        "g_kv_lat": row(g_kv_lat), "w_k": w_k.astype(BF16), "w_v": w_v.astype(BF16),
        "w_o_a": w_o[:MLA_WIDTH].astype(BF16), "w_o_n": w_o[MLA_WIDTH:].astype(BF16), "g_mix_post": row(g_mix_post),
        "g_ffn_pre": row(g_ffn_pre), "w_ffn_up": w_ffn_up.astype(BF16), "ffn_conv_w": ffn_conv_w.astype(F32),
        "ffn_conv_b": row(ffn_conv_b), "w_ffn_down": w_ffn_down.astype(BF16), "g_ffn_post": row(g_ffn_post),
    }


def _rope_tables(S):
    half = QK_ROPE // 2
    inv = 1.0 / (ROPE_THETA ** (jnp.arange(0, QK_ROPE, 2, dtype=F32) / QK_ROPE))
    ang = jnp.arange(S, dtype=F32)[:, None] * inv[None, :]
    cos, sin = jnp.cos(ang), jnp.sin(ang)
    pad = jnp.zeros((S, LANES - QK_NOPE - QK_ROPE), F32)
    cos_tab = jnp.concatenate([jnp.ones((S, QK_NOPE), F32), cos, cos, pad], axis=1)
    sin_tab = jnp.concatenate([jnp.zeros((S, QK_NOPE), F32), sin, sin, pad], axis=1)
    assert half * 2 == QK_ROPE
    return cos_tab, sin_tab


def _na_tables(rpb):
    o = jnp.arange(NA_WIN_R)
    i = jnp.arange(NA_WIN_R)
    qc = jnp.arange(GRID_W)
    kc = jnp.arange(GRID_W)
    dr = (o[:, None] + i[None, :])[:, None, :, None]
    dc = jnp.clip(kc[None, :] - qc[:, None] + (NA_WIN_C - 1), 0, 2 * NA_WIN_C - 2)[None, :, None, :]
    tab = rpb.astype(F32)[:, dr, dc].reshape(NA_HEADS, NA_WIN_R, GRID_W, NA_WIN_R * GRID_W)
    qs = jnp.clip(qc - NA_WIN_C // 2, 0, GRID_W - NA_WIN_C)
    valid = (kc[None, :] >= qs[:, None]) & (kc[None, :] < qs[:, None] + NA_WIN_C)
    mask = jnp.tile(valid.astype(F32), (1, NA_WIN_R))
    return tab, mask


def _layer(x, wts, bias_tab, mask):
    B, S, _ = x.shape
    cos_tab, sin_tab = _rope_tables(S)
    tm = 512
    q, k, v, nq, nk, nv = _proj(x, wts, cos_tab, sin_tab, tm)
    a = _mla(q, k, v, tq=512, tk=512)
    n = _natten(nq, nk, nv, bias_tab, mask, rt=min(32, S // GRID_W))
    x1 = _oproj(a, n, x, wts, tm)
    return _ffn(x1, wts, tm, fc=256)


def kernel(x_prompt, x_sample, g_mix_pre, w_in, g_q_lat, w_q_up, g_kv_lat, w_kv_up, na_rpb, w_o, g_mix_post,
           g_ffn_pre, w_ffn_up, ffn_conv_w, ffn_conv_b, w_ffn_down, g_ffn_post):
    y_prompt, y_sample = x_prompt, x_sample
    for l in range(g_mix_pre.shape[0]):
        wts = _prep_weights(g_mix_pre[l], w_in[l], g_q_lat[l], w_q_up[l], g_kv_lat[l], w_kv_up[l], w_o[l],
                            g_mix_post[l], g_ffn_pre[l], w_ffn_up[l], ffn_conv_w[l], ffn_conv_b[l],
                            w_ffn_down[l], g_ffn_post[l])
        bias_tab, mask = _na_tables(na_rpb[l])
        y_prompt = _layer(y_prompt, wts, bias_tab, mask)
        y_sample = _layer(y_sample, wts, bias_tab, mask)
    return (y_prompt, y_sample)
```

```python
import functools

import jax
import jax.numpy as jnp
from jax import lax
from jax.experimental import pallas as pl
from jax.experimental.pallas import tpu as pltpu

D_MODEL = 1024
GRID_W = 64
MLA_HEADS = 8
Q_LORA = 256
KV_LORA = 128
QK_NOPE = 64
QK_ROPE = 32
V_DIM = 64
ROPE_THETA = 10000.0
NA_HEADS = 8
NA_DIM = 64
NA_WIN_R = 8
NA_WIN_C = 16
NA_WIDTH = NA_HEADS * NA_DIM
MLA_WIDTH = MLA_HEADS * V_DIM
D_FF = 2816
EPS = 1e-6
NEG_INF = -1e30

LANES = 128
HALO = 8
VMEM_LIMIT = 56 * 1024 * 1024

BF16 = jnp.bfloat16
F32 = jnp.float32


def _rms(x, g):
    return x * lax.rsqrt(jnp.mean(x * x, axis=-1, keepdims=True) + EPS) * g


def _dot(a, b):
    return jnp.dot(a, b, preferred_element_type=F32)


def _dot_nt(a, b):
    return lax.dot_general(a, b, (((1,), (1,)), ((), ())), preferred_element_type=F32)


def _const_spec(shape):
    return pl.BlockSpec(shape, lambda *_: (0,) * len(shape))


def _proj_kernel(x_ref, g_pre_ref, w_lat_ref, w_nq_ref, w_nk_ref, w_nv_ref, g_q_ref, w_qt_ref, w_qrott_ref,
                 g_kv_ref, w_k_ref, w_vt_ref, cos_ref, sin_ref, cost_ref, sint_ref,
                 qt_ref, k_ref, vt_ref, nq_ref, nk_ref, nv_ref):
    h = _rms(x_ref[0], g_pre_ref[...]).astype(BF16)
    tm = h.shape[0]
    cos = cos_ref[...]
    sin = sin_ref[...]
    cost = cost_ref[...]
    sint = sint_ref[...]

    z = _dot(h, w_lat_ref[...])
    c_q = z[:, :Q_LORA]
    c_kv = z[:, Q_LORA:Q_LORA + KV_LORA]
    k_r = z[:, Q_LORA + KV_LORA:Q_LORA + KV_LORA + LANES]
    k_rrot = z[:, Q_LORA + KV_LORA + LANES:]
    k_pe = k_r * cos + k_rrot * sin

    cqn = _rms(c_q, g_q_ref[...])
    cqnt = cqn.T.astype(BF16)
    qmt = _dot(w_qt_ref[...], cqnt)
    qrt = _dot(w_qrott_ref[...], cqnt)
    ckvn = _rms(c_kv, g_kv_ref[...])
    kn = _dot(ckvn.astype(BF16), w_k_ref[...])
    vvt = _dot(w_vt_ref[...], ckvn.T.astype(BF16))
    ones_row = lax.broadcasted_iota(jnp.int32, (LANES, tm), 0) == V_DIM
    for hd in range(MLA_HEADS):
        sl = slice(hd * LANES, (hd + 1) * LANES)
        qt_ref[0, hd] = (qmt[sl] * cost + qrt[sl] * sint).astype(BF16)
        k_ref[0, hd] = (kn[:, sl] + k_pe).astype(BF16)
        vt_ref[0, hd, 0] = jnp.where(ones_row, 1.0, vvt[sl]).astype(BF16)

    nq = _dot(h, w_nq_ref[...]) * (NA_DIM ** -0.5)
    nk = _dot(h, w_nk_ref[...])
    nv = _dot(h, w_nv_ref[...])
    for p in range(NA_HEADS // 2):
        sl = slice(p * LANES, (p + 1) * LANES)
        nq_ref[0, p] = nq[:, sl].astype(BF16)
        nk_ref[0, p] = nk[:, sl].astype(BF16)
        nv_ref[0, p] = nv[:, sl].astype(BF16)


def _proj(x, wts, rope, tm):
    B, S, _ = x.shape
    nt = S // tm
    hp = NA_HEADS // 2
    head_out = lambda n: pl.BlockSpec((1, n, tm, LANES), lambda b, i: (b, 0, i, 0))
    names = ("g_mix_pre", "w_lat", "w_nq", "w_nk", "w_nv", "g_q_lat", "w_qt", "w_qrott", "g_kv_lat", "w_k", "w_vt")
    consts = [wts[n] for n in names]
    cos_tab, sin_tab, cos_t, sin_t = rope
    return pl.pallas_call(
        _proj_kernel,
        grid=(B, nt),
        in_specs=[pl.BlockSpec((1, tm, D_MODEL), lambda b, i: (b, i, 0))]
        + [_const_spec(c.shape) for c in consts]
        + [pl.BlockSpec((tm, LANES), lambda b, i: (i, 0))] * 2
        + [pl.BlockSpec((LANES, tm), lambda b, i: (0, i))] * 2,
        out_specs=[pl.BlockSpec((1, MLA_HEADS, LANES, tm), lambda b, i: (b, 0, 0, i)),
                   head_out(MLA_HEADS),
                   pl.BlockSpec((1, MLA_HEADS, 1, LANES, tm), lambda b, i: (b, 0, i, 0, 0)),
                   head_out(hp), head_out(hp), head_out(hp)],
        out_shape=[jax.ShapeDtypeStruct((B, MLA_HEADS, LANES, S), BF16),
                   jax.ShapeDtypeStruct((B, MLA_HEADS, S, LANES), BF16),
                   jax.ShapeDtypeStruct((B, MLA_HEADS, nt, LANES, tm), BF16)]
        + [jax.ShapeDtypeStruct((B, hp, S, LANES), BF16)] * 3,
        compiler_params=pltpu.CompilerParams(
            dimension_semantics=("parallel", "parallel"), vmem_limit_bytes=VMEM_LIMIT),
        name="proj",
    )(x, *consts, cos_tab, sin_tab, cos_t, sin_t)


MLA_STRIP = 32


def _mla_kernel(qt_ref, k_ref, vt_ref, o_ref, s_ref, p_ref, acc_ref, *, log2_scale):
    tq = qt_ref.shape[3]
    tk = vt_ref.shape[4]
    nk = vt_ref.shape[2]
    assert nk % 2 == 0 and tk % MLA_STRIP == 0

    def scores(c, hd, slot):
        ks = pl.ds(pl.multiple_of(c * tk, tk), tk)
        s_ref[slot, hd] = _dot(k_ref[0, hd, ks, :], qt_ref[0, hd])

    def numerators(hd, slot, m):
        mx = s_ref[slot, hd, 0:MLA_STRIP, :]
        for r in range(MLA_STRIP, tk, MLA_STRIP):
            mx = jnp.maximum(mx, s_ref[slot, hd, r:r + MLA_STRIP, :])
        m_new = jnp.maximum(m, jnp.max(mx, axis=0, keepdims=True))
        alpha = jnp.exp2((m - m_new) * log2_scale)
        for r in range(0, tk, MLA_STRIP):
            st = s_ref[slot, hd, r:r + MLA_STRIP, :]
            p_ref[slot, hd, r:r + MLA_STRIP, :] = jnp.exp2((st - m_new) * log2_scale).astype(BF16)
        return m_new, alpha

    def accumulate(c, hd, slot, alpha):
        acc_ref[hd] = alpha * acc_ref[hd] + _dot(vt_ref[0, hd, c], p_ref[slot, hd])

    def chunk(c, cur, carry):
        nxt = 1 - cur
        out = []
        for hd in range(2):
            m, alpha_prev = carry[hd]
            scores(jnp.minimum(c + 1, nk - 1), hd, nxt)
            m_new, alpha = numerators(hd, cur, m)
            accumulate(jnp.maximum(c - 1, 0), hd, nxt, alpha_prev)
            out.append((m_new, alpha))
        return tuple(out)

    def step(j, carry):
        return chunk(2 * j + 1, 1, chunk(2 * j, 0, carry))

    for hd in range(2):
        scores(0, hd, 0)
        p_ref[1, hd] = jnp.zeros((tk, tq), BF16)
        acc_ref[hd] = jnp.zeros((LANES, tq), F32)
    init = tuple((jnp.full((1, tq), -jnp.inf, F32), jnp.ones((1, tq), F32)) for _ in range(2))
    fin = lax.fori_loop(0, nk // 2, step, init)
    outs = []
    for hd in range(2):
        accumulate(nk - 1, hd, 1, fin[hd][1])
        acc = acc_ref[hd]
        outs.append(acc[:V_DIM] / acc[V_DIM:V_DIM + 1])
    o_ref[0] = jnp.concatenate(outs, axis=0).T.astype(o_ref.dtype)


def _mla(qt, k, vt, tq):
    B, H, S, _ = k.shape
    nk, tk = vt.shape[2], vt.shape[4]
    scale = (QK_NOPE + QK_ROPE) ** -0.5
    kern = functools.partial(_mla_kernel, log2_scale=scale * 1.4426950408889634)
    return pl.pallas_call(
        kern,
        grid=(B, H // 2, S // tq),
        in_specs=[pl.BlockSpec((1, 2, LANES, tq), lambda b, p, i: (b, p, 0, i)),
                  pl.BlockSpec((1, 2, S, LANES), lambda b, p, i: (b, p, 0, 0)),
                  pl.BlockSpec((1, 2, nk, LANES, tk), lambda b, p, i: (b, p, 0, 0, 0))],
        out_specs=pl.BlockSpec((1, tq, LANES), lambda b, p, i: (b, i, p)),
        out_shape=jax.ShapeDtypeStruct((B, S, MLA_WIDTH), BF16),
        scratch_shapes=[pltpu.VMEM((2, 2, tk, tq), F32), pltpu.VMEM((2, 2, tk, tq), BF16),
                        pltpu.VMEM((2, LANES, tq), F32)],
        compiler_params=pltpu.CompilerParams(
            dimension_semantics=("parallel", "parallel", "arbitrary"), vmem_limit_bytes=VMEM_LIMIT),
        name="mla_flash",
    )(qt, k, vt)


NA_QROWS = 4
NA_BROWS = NA_QROWS + NA_WIN_R
NA_QTOK = NA_QROWS * GRID_W
NA_BTOK = NA_BROWS * GRID_W


def _na_kernel(q_ref, k_ref, v_ref, tab_ref, o_ref, *, rows, nblk):
    i = pl.program_id(2)
    lane = lax.broadcasted_iota(jnp.int32, (NA_QTOK, LANES), 1)
    low = lane < NA_DIM

    def block(j, carry):
        r0 = (i * nblk + j) * NA_QROWS
        b0 = jnp.clip(r0 - NA_WIN_R // 2, 0, rows - NA_BROWS)
        var = jnp.where(r0 == 0, 0, jnp.where(r0 == rows - NA_QROWS, 2, 1))
        qs = pl.ds(pl.multiple_of(j * NA_QTOK, NA_QTOK), NA_QTOK)
        ks = pl.ds(pl.multiple_of(b0 * GRID_W, GRID_W), NA_BTOK)
        qp = q_ref[0, 0, qs, :]
        kb = k_ref[0, 0, ks, :]
        vb = v_ref[0, 0, ks, :]
        outs = []
        for hd in range(2):
            qh = jnp.where(low if hd == 0 else ~low, qp, jnp.zeros_like(qp))
            s = _dot_nt(qh, kb)
            t = tab_ref[var, hd]
            s = jnp.where(t == -jnp.inf, NEG_INF, s + t)
            m = jnp.max(s, axis=-1, keepdims=True)
            p = jnp.exp(s - m)
            l = jnp.sum(p, axis=-1, keepdims=True)
            outs.append(_dot(p.astype(BF16), vb) / l)
        o_ref[0, qs, :] = jnp.where(low, outs[0], outs[1]).astype(o_ref.dtype)
        return carry

    lax.fori_loop(0, nblk, block, 0)


def _natten(q, k, v, tab, nblk):
    B, HP, S, _ = q.shape
    rows = S // GRID_W
    assert rows >= NA_BROWS and rows % (NA_QROWS * nblk) == 0
    kern = functools.partial(_na_kernel, rows=rows, nblk=nblk)
    return pl.pallas_call(
        kern,
        grid=(HP, B, rows // (NA_QROWS * nblk)),
        in_specs=[pl.BlockSpec((1, 1, nblk * NA_QTOK, LANES), lambda p, b, i: (b, p, i, 0)),
                  pl.BlockSpec((1, 1, S, LANES), lambda p, b, i: (b, p, 0, 0)),
                  pl.BlockSpec((1, 1, S, LANES), lambda p, b, i: (b, p, 0, 0)),
                  pl.BlockSpec((3, 2, NA_QTOK, NA_BTOK), lambda p, b, i: (0, p, 0, 0))],
        out_specs=pl.BlockSpec((1, nblk * NA_QTOK, LANES), lambda p, b, i: (b, i, p)),
        out_shape=jax.ShapeDtypeStruct((B, S, NA_WIDTH), BF16),
        compiler_params=pltpu.CompilerParams(
            dimension_semantics=("parallel", "parallel", "arbitrary"), vmem_limit_bytes=VMEM_LIMIT),
        name="natten",
    )(q, k, v, tab)


def _oproj_kernel(a_ref, n_ref, x_ref, wa_ref, wn_ref, g_ref, o_ref):
    mix = _dot(a_ref[0], wa_ref[...]) + _dot(n_ref[0], wn_ref[...])
    o_ref[0] = x_ref[0] + _rms(mix, g_ref[...])


def _oproj(a, n, x, wts, tm):
    B, S, _ = x.shape
    tok = lambda w: pl.BlockSpec((1, tm, w), lambda b, i: (b, i, 0))
    consts = [wts["w_o_a"], wts["w_o_n"], wts["g_mix_post"]]
    return pl.pallas_call(
        _oproj_kernel,
        grid=(B, S // tm),
        in_specs=[tok(MLA_WIDTH), tok(NA_WIDTH), tok(D_MODEL)] + [_const_spec(c.shape) for c in consts],
        out_specs=tok(D_MODEL),
        out_shape=jax.ShapeDtypeStruct((B, S, D_MODEL), F32),
        compiler_params=pltpu.CompilerParams(
            dimension_semantics=("parallel", "parallel"), vmem_limit_bytes=VMEM_LIMIT),
        name="out_proj",
    )(a, n, x, *consts)


def _gelu_tanh(x):
    return 0.5 * x * (1.0 + jnp.tanh(0.7978845608028654 * (x + 0.044715 * (x * x * x))))


def _ffn_kernel(x_ref, prev_ref, next_ref, g_pre_ref, w_up_ref, cw_ref, cb_ref, w_dn_ref, g_post_ref, o_ref, *, fc):
    i = pl.program_id(1)
    last = pl.num_programs(1) - 1
    x = x_ref[0]
    tm = x.shape[0]
    prev = jnp.where(i > 0, prev_ref[0], 0.0)
    nxt = jnp.where(i < last, next_ref[0], 0.0)
    xh = jnp.concatenate([prev, x, nxt], axis=0)
    hn = _rms(xh, g_pre_ref[...]).astype(BF16)
    n_ext = tm + 2 * HALO

    def conv(hx, col):
        w = cw_ref[:, col:col + fc]
        left = pltpu.roll(hx, 1, 0)[HALO:HALO + tm]
        right = pltpu.roll(hx, n_ext - 1, 0)[HALO:HALO + tm]
        return cb_ref[:, col:col + fc] + left * w[0:1] + hx[HALO:HALO + tm] * w[1:2] + right * w[2:3]

    acc = jnp.zeros((tm, D_MODEL), F32)
    for c in range(D_FF // fc):
        cg = c * fc
        cu = D_FF + c * fc
        g = conv(_dot(hn, w_up_ref[:, cg:cg + fc]), cg)
        u = conv(_dot(hn, w_up_ref[:, cu:cu + fc]), cu)
        act = (_gelu_tanh(g) * u).astype(BF16)
        acc = acc + _dot(act, w_dn_ref[cg:cg + fc, :])
    o_ref[0] = x + _rms(acc, g_post_ref[...])


def _ffn(x, wts, tm, fc):
    B, S, _ = x.shape
    nb = tm // HALO
    kern = functools.partial(_ffn_kernel, fc=fc)
    consts = [wts["g_ffn_pre"], wts["w_ffn_up"], wts["ffn_conv_w"], wts["ffn_conv_b"], wts["w_ffn_down"],
              wts["g_ffn_post"]]
    return pl.pallas_call(
        kern,
        grid=(B, S // tm),
        in_specs=[pl.BlockSpec((1, tm, D_MODEL), lambda b, i: (b, i, 0)),
                  pl.BlockSpec((1, HALO, D_MODEL), lambda b, i: (b, jnp.maximum(i * nb - 1, 0), 0)),
                  pl.BlockSpec((1, HALO, D_MODEL), lambda b, i: (b, jnp.minimum((i + 1) * nb, S // HALO - 1), 0))]
        + [pl.BlockSpec(c.shape, lambda b, i, nd=c.ndim: (0,) * nd, pipeline_mode=pl.Buffered(1)) for c in consts],
        out_specs=pl.BlockSpec((1, tm, D_MODEL), lambda b, i: (b, i, 0)),
        out_shape=jax.ShapeDtypeStruct((B, S, D_MODEL), F32),
        compiler_params=pltpu.CompilerParams(
            dimension_semantics=("parallel", "parallel"), vmem_limit_bytes=VMEM_LIMIT),
        name="conv_ffn",
    )(x, x, x, *consts)


def _prep_weights(g_mix_pre, w_in, g_q_lat, w_q_up, g_kv_lat, w_kv_up, w_o, g_mix_post,
                  g_ffn_pre, w_ffn_up, ffn_conv_w, ffn_conv_b, w_ffn_down, g_ffn_post):
    half = QK_ROPE // 2
    o1, o2, o3 = Q_LORA, Q_LORA + KV_LORA, Q_LORA + KV_LORA + QK_ROPE
    row = lambda g: g.reshape(1, -1).astype(F32)

    w_kr = w_in[:, o2:o3]
    zl = jnp.zeros((D_MODEL, QK_NOPE), F32)
    zr = jnp.zeros((D_MODEL, LANES - QK_NOPE - QK_ROPE), F32)
    w_kr_blk = jnp.concatenate([zl, w_kr, zr], axis=1)
    w_krrot_blk = jnp.concatenate([zl, -w_kr[:, half:], w_kr[:, :half], zr], axis=1)
    w_lat = jnp.concatenate([w_in[:, :o2], w_kr_blk, w_krrot_blk], axis=1)

    wq = w_q_up.reshape(Q_LORA, MLA_HEADS, QK_NOPE + QK_ROPE)
    zq = jnp.zeros((Q_LORA, MLA_HEADS, LANES - QK_NOPE - QK_ROPE), F32)
    w_q = jnp.concatenate([wq, zq], axis=2).reshape(Q_LORA, MLA_HEADS * LANES)
    w_qrot = jnp.concatenate([jnp.zeros((Q_LORA, MLA_HEADS, QK_NOPE), F32), -wq[:, :, QK_NOPE + half:],
                              wq[:, :, QK_NOPE:QK_NOPE + half], zq], axis=2).reshape(Q_LORA, MLA_HEADS * LANES)

    wkv = w_kv_up.reshape(KV_LORA, MLA_HEADS, QK_NOPE + V_DIM)
    w_k = jnp.concatenate([wkv[:, :, :QK_NOPE], jnp.zeros((KV_LORA, MLA_HEADS, LANES - QK_NOPE), F32)],
                          axis=2).reshape(KV_LORA, MLA_HEADS * LANES)
    w_v = jnp.concatenate([wkv[:, :, QK_NOPE:], jnp.zeros((KV_LORA, MLA_HEADS, LANES - V_DIM), F32)],
                          axis=2).reshape(KV_LORA, MLA_HEADS * LANES)

    return {
        "g_mix_pre": row(g_mix_pre), "w_lat": w_lat.astype(BF16),
        "w_nq": w_in[:, o3:o3 + NA_WIDTH].astype(BF16),
        "w_nk": w_in[:, o3 + NA_WIDTH:o3 + 2 * NA_WIDTH].astype(BF16),
        "w_nv": w_in[:, o3 + 2 * NA_WIDTH:].astype(BF16),
        "g_q_lat": row(g_q_lat), "w_qt": w_q.T.astype(BF16), "w_qrott": w_qrot.T.astype(BF16),
        "g_kv_lat": row(g_kv_lat), "w_k": w_k.astype(BF16), "w_vt": w_v.T.astype(BF16),
        "w_o_a": w_o[:MLA_WIDTH].astype(BF16), "w_o_n": w_o[MLA_WIDTH:].astype(BF16), "g_mix_post": row(g_mix_post),
        "g_ffn_pre": row(g_ffn_pre), "w_ffn_up": w_ffn_up.astype(BF16), "ffn_conv_w": ffn_conv_w.astype(F32),
        "ffn_conv_b": row(ffn_conv_b), "w_ffn_down": w_ffn_down.astype(BF16), "g_ffn_post": row(g_ffn_post),
    }


def _rope_tables(S):
    inv = 1.0 / (ROPE_THETA ** (jnp.arange(0, QK_ROPE, 2, dtype=F32) / QK_ROPE))
    ang = jnp.arange(S, dtype=F32)[:, None] * inv[None, :]
    cos, sin = jnp.cos(ang), jnp.sin(ang)
    pad = jnp.zeros((S, LANES - QK_NOPE - QK_ROPE), F32)
    cos_tab = jnp.concatenate([jnp.ones((S, QK_NOPE), F32), cos, cos, pad], axis=1)
    sin_tab = jnp.concatenate([jnp.zeros((S, QK_NOPE), F32), sin, sin, pad], axis=1)
    return cos_tab, sin_tab, cos_tab.T, sin_tab.T


def _na_tables(rpb):
    qc = jnp.arange(GRID_W)[:, None]
    kc = jnp.arange(GRID_W)[None, :]
    dc = jnp.clip(kc - qc + (NA_WIN_C - 1), 0, 2 * NA_WIN_C - 2)
    qs = jnp.clip(qc - NA_WIN_C // 2, 0, GRID_W - NA_WIN_C)
    col_valid = (kc >= qs) & (kc < qs + NA_WIN_C)
    rpb = rpb.astype(F32)
    tile = jnp.zeros(rpb.shape[:2] + (GRID_W, GRID_W), F32)
    for e in range(2 * NA_WIN_C - 1):
        tile = jnp.where(dc == e, rpb[:, :, e][:, :, None, None], tile)
    tile = jnp.where(col_valid, tile, -jnp.inf)
    outside = jnp.full((NA_HEADS, GRID_W, GRID_W), -jnp.inf, F32)
    variants = []
    for shift, lo in ((0, lambda j: 0), (-NA_WIN_R // 2, lambda j: j), (-NA_WIN_R, lambda j: NA_BROWS - NA_WIN_R)):
        qrows = []
        for j in range(NA_QROWS):
            blocks = [tile[:, i - j + shift + NA_WIN_R - 1] if lo(j) <= i < lo(j) + NA_WIN_R else outside
                      for i in range(NA_BROWS)]
            qrows.append(jnp.concatenate(blocks, axis=2))
        variants.append(jnp.concatenate(qrows, axis=1))
    return jnp.stack(variants)


def _layer(x, wts, na_tab):
    B, S, _ = x.shape
    tm = 512
    qt, k, vt, nq, nk, nv = _proj(x, wts, _rope_tables(S), tm)
    a = _mla(qt, k, vt, tq=512)
    n = _natten(nq, nk, nv, na_tab, nblk=min(8, S // NA_QTOK))
    x1 = _oproj(a, n, x, wts, tm)
    return _ffn(x1, wts, tm, fc=256)


def kernel(x_prompt, x_sample, g_mix_pre, w_in, g_q_lat, w_q_up, g_kv_lat, w_kv_up, na_rpb, w_o, g_mix_post,
           g_ffn_pre, w_ffn_up, ffn_conv_w, ffn_conv_b, w_ffn_down, g_ffn_post):
    y_prompt, y_sample = x_prompt, x_sample
    for l in range(g_mix_pre.shape[0]):
        wts = _prep_weights(g_mix_pre[l], w_in[l], g_q_lat[l], w_q_up[l], g_kv_lat[l], w_kv_up[l], w_o[l],
                            g_mix_post[l], g_ffn_pre[l], w_ffn_up[l], ffn_conv_w[l], ffn_conv_b[l],
                            w_ffn_down[l], g_ffn_post[l])
        na_tab = _na_tables(na_rpb[l])
        y_prompt = _layer(y_prompt, wts, na_tab)
        y_sample = _layer(y_sample, wts, na_tab)
    return (y_prompt, y_sample)
```

```python
import functools

import jax
import jax.numpy as jnp
from jax import lax
from jax.experimental import pallas as pl
from jax.experimental.pallas import tpu as pltpu

D_MODEL = 1024
GRID_W = 64
MLA_HEADS = 8
Q_LORA = 256
KV_LORA = 128
QK_NOPE = 64
QK_ROPE = 32
V_DIM = 64
ROPE_THETA = 10000.0
NA_HEADS = 8
NA_DIM = 64
NA_WIN_R = 8
NA_WIN_C = 16
NA_WIDTH = NA_HEADS * NA_DIM
MLA_WIDTH = MLA_HEADS * V_DIM
D_FF = 2816
EPS = 1e-6
NEG_INF = -1e30

LANES = 128
V_ROWS = 80
Q_SCALE = (QK_NOPE + QK_ROPE) ** -0.5 * 1.4426950408889634
HALO = 8
VMEM_LIMIT = 56 * 1024 * 1024

BF16 = jnp.bfloat16
F32 = jnp.float32


def _rms(x, g):
    return x * lax.rsqrt(jnp.mean(x * x, axis=-1, keepdims=True) + EPS) * g


def _dot(a, b):
    return jnp.dot(a, b, preferred_element_type=F32)


def _dot_nt(a, b):
    return lax.dot_general(a, b, (((1,), (1,)), ((), ())), preferred_element_type=F32)


def _const_spec(shape):
    return pl.BlockSpec(shape, lambda *_: (0,) * len(shape))


def _proj_kernel(x_ref, g_pre_ref, w_lat_ref, w_nq_ref, w_nk_ref, w_nv_ref, g_q_ref, w_qt_ref, w_qrott_ref,
                 g_kv_ref, w_k_ref, w_vt_ref, cos_ref, sin_ref, cost_ref, sint_ref,
                 qt_ref, k_ref, vt_ref, nq_ref, nk_ref, nv_ref):
    h = _rms(x_ref[0], g_pre_ref[...]).astype(BF16)
    tm = h.shape[0]
    cos = cos_ref[...]
    sin = sin_ref[...]
    cost = cost_ref[...]
    sint = sint_ref[...]

    z = _dot(h, w_lat_ref[...])
    c_q = z[:, :Q_LORA]
    c_kv = z[:, Q_LORA:Q_LORA + KV_LORA]
    k_r = z[:, Q_LORA + KV_LORA:Q_LORA + KV_LORA + LANES]
    k_rrot = z[:, Q_LORA + KV_LORA + LANES:]
    k_pe = k_r * cos + k_rrot * sin

    cqn = _rms(c_q, g_q_ref[...])
    cqnt = cqn.T.astype(BF16)
    qmt = _dot(w_qt_ref[...], cqnt)
    qrt = _dot(w_qrott_ref[...], cqnt)
    ckvn = _rms(c_kv, g_kv_ref[...])
    kn = _dot(ckvn.astype(BF16), w_k_ref[...])
    vvt = _dot(w_vt_ref[...], ckvn.T.astype(BF16))
    ones_row = lax.broadcasted_iota(jnp.int32, (V_ROWS, tm), 0) == V_DIM
    for hd in range(MLA_HEADS):
        sl = slice(hd * LANES, (hd + 1) * LANES)
        qt_ref[0, hd] = ((qmt[sl] * cost + qrt[sl] * sint) * Q_SCALE).astype(BF16)
        k_ref[0, hd] = (kn[:, sl] + k_pe).astype(BF16)
        vt_ref[0, hd, 0] = jnp.where(ones_row, 1.0, vvt[hd * V_ROWS:(hd + 1) * V_ROWS]).astype(BF16)

    nq = _dot(h, w_nq_ref[...]) * (NA_DIM ** -0.5)
    nk = _dot(h, w_nk_ref[...])
    nv = _dot(h, w_nv_ref[...])
    for p in range(NA_HEADS // 2):
        sl = slice(p * LANES, (p + 1) * LANES)
        nq_ref[0, p] = nq[:, sl].astype(BF16)
        nk_ref[0, p] = nk[:, sl].astype(BF16)
        nv_ref[0, p] = nv[:, sl].astype(BF16)


def _proj(x, wts, rope, tm, tk):
    B, S, _ = x.shape
    nt = S // tm
    per = tk // tm
    hp = NA_HEADS // 2
    head_out = lambda n: pl.BlockSpec((1, n, tm, LANES), lambda b, i: (b, 0, i, 0))
    names = ("g_mix_pre", "w_lat", "w_nq", "w_nk", "w_nv", "g_q_lat", "w_qt", "w_qrott", "g_kv_lat", "w_k", "w_vt")
    consts = [wts[n] for n in names]
    cos_tab, sin_tab, cos_t, sin_t = rope
    return pl.pallas_call(
        _proj_kernel,
        grid=(B, nt),
        in_specs=[pl.BlockSpec((1, tm, D_MODEL), lambda b, i: (b, i, 0))]
        + [_const_spec(c.shape) for c in consts]
        + [pl.BlockSpec((tm, LANES), lambda b, i: (i, 0))] * 2
        + [pl.BlockSpec((LANES, tm), lambda b, i: (0, i))] * 2,
        out_specs=[pl.BlockSpec((1, MLA_HEADS, LANES, tm), lambda b, i: (b, 0, 0, i)),
                   head_out(MLA_HEADS),
                   pl.BlockSpec((1, MLA_HEADS, 1, V_ROWS, tm), lambda b, i: (b, 0, i // per, 0, i % per)),
                   head_out(hp), head_out(hp), head_out(hp)],
        out_shape=[jax.ShapeDtypeStruct((B, MLA_HEADS, LANES, S), BF16),
                   jax.ShapeDtypeStruct((B, MLA_HEADS, S, LANES), BF16),
                   jax.ShapeDtypeStruct((B, MLA_HEADS, S // tk, V_ROWS, tk), BF16)]
        + [jax.ShapeDtypeStruct((B, hp, S, LANES), BF16)] * 3,
        compiler_params=pltpu.CompilerParams(
            dimension_semantics=("parallel", "parallel"), vmem_limit_bytes=VMEM_LIMIT),
        name="proj",
    )(x, *consts, cos_tab, sin_tab, cos_t, sin_t)


MLA_STRIP = 32


def _mla_kernel(qt_ref, k_ref, vt_ref, o_ref, s_ref, p_ref, acc_ref):
    tq = qt_ref.shape[3]
    tk = vt_ref.shape[4]
    nk = vt_ref.shape[2]
    assert nk % 2 == 0 and tk % MLA_STRIP == 0

    def scores(c, hd, slot):
        ks = pl.ds(pl.multiple_of(c * tk, tk), tk)
        st = _dot(k_ref[0, hd, ks, :], qt_ref[0, hd])
        s_ref[slot, hd] = st
        mx = st[0:MLA_STRIP]
        for r in range(MLA_STRIP, tk, MLA_STRIP):
            mx = jnp.maximum(mx, st[r:r + MLA_STRIP])
        return jnp.max(mx, axis=0, keepdims=True)

    def numerators(hd, slot, m, mx):
        m_new = jnp.maximum(m, mx)
        alpha = jnp.exp2(m - m_new)
        for r in range(0, tk, MLA_STRIP):
            st = s_ref[slot, hd, r:r + MLA_STRIP, :]
            p_ref[slot, hd, r:r + MLA_STRIP, :] = jnp.exp2(st - m_new).astype(BF16)
        return m_new, alpha

    def accumulate(c, hd, slot, alpha):
        acc_ref[hd] = alpha * acc_ref[hd] + _dot(vt_ref[0, hd, c], p_ref[slot, hd])

    def chunk(c, cur, carry):
        nxt = 1 - cur
        out = []
        for hd in range(2):
            m, alpha_prev, mx = carry[hd]
            mx_next = scores(jnp.minimum(c + 1, nk - 1), hd, nxt)
            m_new, alpha = numerators(hd, cur, m, mx)
            accumulate(jnp.maximum(c - 1, 0), hd, nxt, alpha_prev)
            out.append((m_new, alpha, mx_next))
        return tuple(out)

    def step(c, carry):
        return lax.cond(c % 2 == 0, lambda: chunk(c, 0, carry), lambda: chunk(c, 1, carry))

    init = []
    for hd in range(2):
        mx0 = scores(0, hd, 0)
        p_ref[1, hd] = jnp.zeros((tk, tq), BF16)
        acc_ref[hd] = jnp.zeros((V_ROWS, tq), F32)
        init.append((jnp.full((1, tq), -jnp.inf, F32), jnp.ones((1, tq), F32), mx0))
    fin = lax.fori_loop(0, nk, step, tuple(init))
    outs = []
    for hd in range(2):
        accumulate(nk - 1, hd, 1, fin[hd][1])
        acc = acc_ref[hd]
        outs.append(acc[:V_DIM] / acc[V_DIM:V_DIM + 1])
    o_ref[0] = jnp.concatenate(outs, axis=0).T.astype(o_ref.dtype)


def _mla(qt, k, vt, tq):
    B, H, S, _ = k.shape
    nk, tk = vt.shape[2], vt.shape[4]
    return pl.pallas_call(
        _mla_kernel,
        grid=(B, H // 2, S // tq),
        in_specs=[pl.BlockSpec((1, 2, LANES, tq), lambda b, p, i: (b, p, 0, i)),
                  pl.BlockSpec((1, 2, S, LANES), lambda b, p, i: (b, p, 0, 0)),
                  pl.BlockSpec((1, 2, nk, V_ROWS, tk), lambda b, p, i: (b, p, 0, 0, 0))],
        out_specs=pl.BlockSpec((1, tq, LANES), lambda b, p, i: (b, i, p)),
        out_shape=jax.ShapeDtypeStruct((B, S, MLA_WIDTH), BF16),
        scratch_shapes=[pltpu.VMEM((2, 2, tk, tq), F32), pltpu.VMEM((2, 2, tk, tq), BF16),
                        pltpu.VMEM((2, V_ROWS, tq), F32)],
        compiler_params=pltpu.CompilerParams(
            dimension_semantics=("parallel", "parallel", "arbitrary"), vmem_limit_bytes=VMEM_LIMIT),
        name="mla_flash",
    )(qt, k, vt)


NA_QROWS = 4
NA_BROWS = NA_QROWS + NA_WIN_R
NA_QTOK = NA_QROWS * GRID_W
NA_BTOK = NA_BROWS * GRID_W


def _na_kernel(q_ref, k_ref, v_ref, tab_ref, o_ref, *, rows, nblk):
    i = pl.program_id(2)
    lane = lax.broadcasted_iota(jnp.int32, (NA_QTOK, LANES), 1)
    low = lane < NA_DIM

    def block(j, carry):
        r0 = (i * nblk + j) * NA_QROWS
        b0 = jnp.clip(r0 - NA_WIN_R // 2, 0, rows - NA_BROWS)
        var = jnp.where(r0 == 0, 0, jnp.where(r0 == rows - NA_QROWS, 2, 1))
        qs = pl.ds(pl.multiple_of(j * NA_QTOK, NA_QTOK), NA_QTOK)
        ks = pl.ds(pl.multiple_of(b0 * GRID_W, GRID_W), NA_BTOK)
        qp = q_ref[0, 0, qs, :]
        kb = k_ref[0, 0, ks, :]
        vb = v_ref[0, 0, ks, :]
        outs = []
        for hd in range(2):
            qh = jnp.where(low if hd == 0 else ~low, qp, jnp.zeros_like(qp))
            s = _dot_nt(qh, kb)
            t = tab_ref[var, hd]
            s = jnp.where(t == -jnp.inf, NEG_INF, s + t)
            m = jnp.max(s, axis=-1, keepdims=True)
            p = jnp.exp(s - m)
            l = jnp.sum(p, axis=-1, keepdims=True)
            outs.append(_dot(p.astype(BF16), vb) / l)
        o_ref[0, qs, :] = jnp.where(low, outs[0], outs[1]).astype(o_ref.dtype)
        return carry

    lax.fori_loop(0, nblk, block, 0)


def _natten(q, k, v, tab, nblk):
    B, HP, S, _ = q.shape
    rows = S // GRID_W
    assert rows >= NA_BROWS and rows % (NA_QROWS * nblk) == 0
    kern = functools.partial(_na_kernel, rows=rows, nblk=nblk)
    return pl.pallas_call(
        kern,
        grid=(HP, B, rows // (NA_QROWS * nblk)),
        in_specs=[pl.BlockSpec((1, 1, nblk * NA_QTOK, LANES), lambda p, b, i: (b, p, i, 0)),
                  pl.BlockSpec((1, 1, S, LANES), lambda p, b, i: (b, p, 0, 0)),
                  pl.BlockSpec((1, 1, S, LANES), lambda p, b, i: (b, p, 0, 0)),
                  pl.BlockSpec((3, 2, NA_QTOK, NA_BTOK), lambda p, b, i: (0, p, 0, 0))],
        out_specs=pl.BlockSpec((1, nblk * NA_QTOK, LANES), lambda p, b, i: (b, i, p)),
        out_shape=jax.ShapeDtypeStruct((B, S, NA_WIDTH), BF16),
        compiler_params=pltpu.CompilerParams(
            dimension_semantics=("parallel", "parallel", "arbitrary"), vmem_limit_bytes=VMEM_LIMIT),
        name="natten",
    )(q, k, v, tab)


def _oproj_kernel(a_ref, n_ref, x_ref, wa_ref, wn_ref, g_ref, o_ref):
    mix = _dot(a_ref[0], wa_ref[...]) + _dot(n_ref[0], wn_ref[...])
    o_ref[0] = x_ref[0] + _rms(mix, g_ref[...])


def _oproj(a, n, x, wts, tm):
    B, S, _ = x.shape
    tok = lambda w: pl.BlockSpec((1, tm, w), lambda b, i: (b, i, 0))
    consts = [wts["w_o_a"], wts["w_o_n"], wts["g_mix_post"]]
    return pl.pallas_call(
        _oproj_kernel,
        grid=(B, S // tm),
        in_specs=[tok(MLA_WIDTH), tok(NA_WIDTH), tok(D_MODEL)] + [_const_spec(c.shape) for c in consts],
        out_specs=tok(D_MODEL),
        out_shape=jax.ShapeDtypeStruct((B, S, D_MODEL), F32),
        compiler_params=pltpu.CompilerParams(
            dimension_semantics=("parallel", "parallel"), vmem_limit_bytes=VMEM_LIMIT),
        name="out_proj",
    )(a, n, x, *consts)


def _gelu_tanh(x):
    return 0.5 * x * (1.0 + jnp.tanh(0.7978845608028654 * (x + 0.044715 * (x * x * x))))


def _ffn_kernel(x_ref, prev_ref, next_ref, g_pre_ref, w_up_ref, cw_ref, cb_ref, w_dn_ref, g_post_ref, o_ref, *, fc):
    i = pl.program_id(1)
    last = pl.num_programs(1) - 1
    x = x_ref[0]
    tm = x.shape[0]
    prev = jnp.where(i > 0, prev_ref[0], 0.0)
    nxt = jnp.where(i < last, next_ref[0], 0.0)
    xh = jnp.concatenate([prev, x, nxt], axis=0)
    hn = _rms(xh, g_pre_ref[...]).astype(BF16)
    n_ext = tm + 2 * HALO

    def conv(hx, col):
        w = cw_ref[:, col:col + fc]
        left = pltpu.roll(hx, 1, 0)[HALO:HALO + tm]
        right = pltpu.roll(hx, n_ext - 1, 0)[HALO:HALO + tm]
        return cb_ref[:, col:col + fc] + left * w[0:1] + hx[HALO:HALO + tm] * w[1:2] + right * w[2:3]

    acc = jnp.zeros((tm, D_MODEL), F32)
    for c in range(D_FF // fc):
        cg = c * fc
        cu = D_FF + c * fc
        g = conv(_dot(hn, w_up_ref[:, cg:cg + fc]), cg)
        u = conv(_dot(hn, w_up_ref[:, cu:cu + fc]), cu)
        act = (_gelu_tanh(g) * u).astype(BF16)
        acc = acc + _dot(act, w_dn_ref[cg:cg + fc, :])
    o_ref[0] = x + _rms(acc, g_post_ref[...])


def _ffn(x, wts, tm, fc):
    B, S, _ = x.shape
    nb = tm // HALO
    kern = functools.partial(_ffn_kernel, fc=fc)
    consts = [wts["g_ffn_pre"], wts["w_ffn_up"], wts["ffn_conv_w"], wts["ffn_conv_b"], wts["w_ffn_down"],
              wts["g_ffn_post"]]
    return pl.pallas_call(
        kern,
        grid=(B, S // tm),
        in_specs=[pl.BlockSpec((1, tm, D_MODEL), lambda b, i: (b, i, 0)),
                  pl.BlockSpec((1, HALO, D_MODEL), lambda b, i: (b, jnp.maximum(i * nb - 1, 0), 0)),
                  pl.BlockSpec((1, HALO, D_MODEL), lambda b, i: (b, jnp.minimum((i + 1) * nb, S // HALO - 1), 0))]
        + [pl.BlockSpec(c.shape, lambda b, i, nd=c.ndim: (0,) * nd, pipeline_mode=pl.Buffered(1)) for c in consts],
        out_specs=pl.BlockSpec((1, tm, D_MODEL), lambda b, i: (b, i, 0)),
        out_shape=jax.ShapeDtypeStruct((B, S, D_MODEL), F32),
        compiler_params=pltpu.CompilerParams(
            dimension_semantics=("parallel", "parallel"), vmem_limit_bytes=VMEM_LIMIT),
        name="conv_ffn",
    )(x, x, x, *consts)


def _prep_weights(g_mix_pre, w_in, g_q_lat, w_q_up, g_kv_lat, w_kv_up, w_o, g_mix_post,
                  g_ffn_pre, w_ffn_up, ffn_conv_w, ffn_conv_b, w_ffn_down, g_ffn_post):
    half = QK_ROPE // 2
    o1, o2, o3 = Q_LORA, Q_LORA + KV_LORA, Q_LORA + KV_LORA + QK_ROPE
    row = lambda g: g.reshape(1, -1).astype(F32)

    w_kr = w_in[:, o2:o3]
    zl = jnp.zeros((D_MODEL, QK_NOPE), F32)
    zr = jnp.zeros((D_MODEL, LANES - QK_NOPE - QK_ROPE), F32)
    w_kr_blk = jnp.concatenate([zl, w_kr, zr], axis=1)
    w_krrot_blk = jnp.concatenate([zl, -w_kr[:, half:], w_kr[:, :half], zr], axis=1)
    w_lat = jnp.concatenate([w_in[:, :o2], w_kr_blk, w_krrot_blk], axis=1)

    wq = w_q_up.reshape(Q_LORA, MLA_HEADS, QK_NOPE + QK_ROPE)
    zq = jnp.zeros((Q_LORA, MLA_HEADS, LANES - QK_NOPE - QK_ROPE), F32)
    w_q = jnp.concatenate([wq, zq], axis=2).reshape(Q_LORA, MLA_HEADS * LANES)
    w_qrot = jnp.concatenate([jnp.zeros((Q_LORA, MLA_HEADS, QK_NOPE), F32), -wq[:, :, QK_NOPE + half:],
                              wq[:, :, QK_NOPE:QK_NOPE + half], zq], axis=2).reshape(Q_LORA, MLA_HEADS * LANES)

    wkv = w_kv_up.reshape(KV_LORA, MLA_HEADS, QK_NOPE + V_DIM)
    w_k = jnp.concatenate([wkv[:, :, :QK_NOPE], jnp.zeros((KV_LORA, MLA_HEADS, LANES - QK_NOPE), F32)],
                          axis=2).reshape(KV_LORA, MLA_HEADS * LANES)
    w_v = jnp.concatenate([wkv[:, :, QK_NOPE:], jnp.zeros((KV_LORA, MLA_HEADS, V_ROWS - V_DIM), F32)],
                          axis=2).reshape(KV_LORA, MLA_HEADS * V_ROWS)

    return {
        "g_mix_pre": row(g_mix_pre), "w_lat": w_lat.astype(BF16),
        "w_nq": w_in[:, o3:o3 + NA_WIDTH].astype(BF16),
        "w_nk": w_in[:, o3 + NA_WIDTH:o3 + 2 * NA_WIDTH].astype(BF16),
        "w_nv": w_in[:, o3 + 2 * NA_WIDTH:].astype(BF16),
        "g_q_lat": row(g_q_lat), "w_qt": w_q.T.astype(BF16), "w_qrott": w_qrot.T.astype(BF16),
        "g_kv_lat": row(g_kv_lat), "w_k": w_k.astype(BF16), "w_vt": w_v.T.astype(BF16),
        "w_o_a": w_o[:MLA_WIDTH].astype(BF16), "w_o_n": w_o[MLA_WIDTH:].astype(BF16), "g_mix_post": row(g_mix_post),
        "g_ffn_pre": row(g_ffn_pre), "w_ffn_up": w_ffn_up.astype(BF16), "ffn_conv_w": ffn_conv_w.astype(F32),
        "ffn_conv_b": row(ffn_conv_b), "w_ffn_down": w_ffn_down.astype(BF16), "g_ffn_post": row(g_ffn_post),
    }


def _rope_tables(S):
    inv = 1.0 / (ROPE_THETA ** (jnp.arange(0, QK_ROPE, 2, dtype=F32) / QK_ROPE))
    ang = jnp.arange(S, dtype=F32)[:, None] * inv[None, :]
    cos, sin = jnp.cos(ang), jnp.sin(ang)
    pad = jnp.zeros((S, LANES - QK_NOPE - QK_ROPE), F32)
    cos_tab = jnp.concatenate([jnp.ones((S, QK_NOPE), F32), cos, cos, pad], axis=1)
    sin_tab = jnp.concatenate([jnp.zeros((S, QK_NOPE), F32), sin, sin, pad], axis=1)
    return cos_tab, sin_tab, cos_tab.T, sin_tab.T


def _na_tables(rpb):
    qc = jnp.arange(GRID_W)[:, None]
    kc = jnp.arange(GRID_W)[None, :]
    dc = jnp.clip(kc - qc + (NA_WIN_C - 1), 0, 2 * NA_WIN_C - 2)
    qs = jnp.clip(qc - NA_WIN_C // 2, 0, GRID_W - NA_WIN_C)
    col_valid = (kc >= qs) & (kc < qs + NA_WIN_C)
    rpb = rpb.astype(F32)
    tile = jnp.zeros(rpb.shape[:2] + (GRID_W, GRID_W), F32)
    for e in range(2 * NA_WIN_C - 1):
        tile = jnp.where(dc == e, rpb[:, :, e][:, :, None, None], tile)
    tile = jnp.where(col_valid, tile, -jnp.inf)
    outside = jnp.full((NA_HEADS, GRID_W, GRID_W), -jnp.inf, F32)
    variants = []
    for shift, lo in ((0, lambda j: 0), (-NA_WIN_R // 2, lambda j: j), (-NA_WIN_R, lambda j: NA_BROWS - NA_WIN_R)):
        qrows = []
        for j in range(NA_QROWS):
            blocks = [tile[:, i - j + shift + NA_WIN_R - 1] if lo(j) <= i < lo(j) + NA_WIN_R else outside
                      for i in range(NA_BROWS)]
            qrows.append(jnp.concatenate(blocks, axis=2))
        variants.append(jnp.concatenate(qrows, axis=1))
    return jnp.stack(variants)


def _layer(x, wts, na_tab):
    B, S, _ = x.shape
    tm = 512
    qt, k, vt, nq, nk, nv = _proj(x, wts, _rope_tables(S), tm, tk=1024)
    a = _mla(qt, k, vt, tq=512)
    n = _natten(nq, nk, nv, na_tab, nblk=min(8, S // NA_QTOK))
    x1 = _oproj(a, n, x, wts, tm)
    return _ffn(x1, wts, tm, fc=256)


def kernel(x_prompt, x_sample, g_mix_pre, w_in, g_q_lat, w_q_up, g_kv_lat, w_kv_up, na_rpb, w_o, g_mix_post,
           g_ffn_pre, w_ffn_up, ffn_conv_w, ffn_conv_b, w_ffn_down, g_ffn_post):
    y_prompt, y_sample = x_prompt, x_sample
    for l in range(g_mix_pre.shape[0]):
        wts = _prep_weights(g_mix_pre[l], w_in[l], g_q_lat[l], w_q_up[l], g_kv_lat[l], w_kv_up[l], w_o[l],
                            g_mix_post[l], g_ffn_pre[l], w_ffn_up[l], ffn_conv_w[l], ffn_conv_b[l],
                            w_ffn_down[l], g_ffn_post[l])
        na_tab = _na_tables(na_rpb[l])
        y_prompt = _layer(y_prompt, wts, na_tab)
        y_sample = _layer(y_sample, wts, na_tab)
    return (y_prompt, y_sample)
```

```python
import functools

import jax
import jax.numpy as jnp
from jax import lax
from jax.experimental import pallas as pl
from jax.experimental.pallas import tpu as pltpu

D_MODEL = 1024
GRID_W = 64
MLA_HEADS = 8
Q_LORA = 256
KV_LORA = 128
QK_NOPE = 64
QK_ROPE = 32
V_DIM = 64
ROPE_THETA = 10000.0
NA_HEADS = 8
NA_DIM = 64
NA_WIN_R = 8
NA_WIN_C = 16
NA_WIDTH = NA_HEADS * NA_DIM
MLA_WIDTH = MLA_HEADS * V_DIM
D_FF = 2816
EPS = 1e-6
NEG_INF = -1e30

LANES = 128
V_ROWS = 80
Q_SCALE = (QK_NOPE + QK_ROPE) ** -0.5 * 1.4426950408889634
HALO = 8
VMEM_LIMIT = 56 * 1024 * 1024

BF16 = jnp.bfloat16
F32 = jnp.float32


def _rms(x, g):
    return x * lax.rsqrt(jnp.mean(x * x, axis=-1, keepdims=True) + EPS) * g


def _dot(a, b):
    return jnp.dot(a, b, preferred_element_type=F32)


def _dot_nt(a, b):
    return lax.dot_general(a, b, (((1,), (1,)), ((), ())), preferred_element_type=F32)


def _const_spec(shape):
    return pl.BlockSpec(shape, lambda *_: (0,) * len(shape))


def _proj_kernel(x_ref, g_pre_ref, w_lat_ref, w_nq_ref, w_nk_ref, w_nv_ref, g_q_ref, w_qt_ref, w_qrott_ref,
                 g_kv_ref, w_k_ref, w_vt_ref, cos_ref, sin_ref, cost_ref, sint_ref,
                 qt_ref, k_ref, vt_ref, nq_ref, nk_ref, nv_ref):
    h = _rms(x_ref[0], g_pre_ref[...]).astype(BF16)
    tm = h.shape[0]
    cos = cos_ref[...]
    sin = sin_ref[...]
    cost = cost_ref[...]
    sint = sint_ref[...]

    z = _dot(h, w_lat_ref[...])
    c_q = z[:, :Q_LORA]
    c_kv = z[:, Q_LORA:Q_LORA + KV_LORA]
    k_r = z[:, Q_LORA + KV_LORA:Q_LORA + KV_LORA + LANES]
    k_rrot = z[:, Q_LORA + KV_LORA + LANES:]
    k_pe = k_r * cos + k_rrot * sin

    cqn = _rms(c_q, g_q_ref[...])
    cqnt = cqn.T.astype(BF16)
    qmt = _dot(w_qt_ref[...], cqnt)
    qrt = _dot(w_qrott_ref[...], cqnt)
    ckvn = _rms(c_kv, g_kv_ref[...])
    kn = _dot(ckvn.astype(BF16), w_k_ref[...])
    vvt = _dot(w_vt_ref[...], ckvn.T.astype(BF16))
    ones_row = lax.broadcasted_iota(jnp.int32, (V_ROWS, tm), 0) == V_DIM
    for hd in range(MLA_HEADS):
        sl = slice(hd * LANES, (hd + 1) * LANES)
        qt_ref[0, hd, 0] = ((qmt[sl] * cost + qrt[sl] * sint) * Q_SCALE).astype(BF16)
        k_ref[0, hd] = (kn[:, sl] + k_pe).astype(BF16)
        vt_ref[0, hd, 0] = jnp.where(ones_row, 1.0, vvt[hd * V_ROWS:(hd + 1) * V_ROWS]).astype(BF16)

    nq = _dot(h, w_nq_ref[...]) * (NA_DIM ** -0.5)
    nk = _dot(h, w_nk_ref[...])
    nv = _dot(h, w_nv_ref[...])
    for p in range(NA_HEADS // 2):
        sl = slice(p * LANES, (p + 1) * LANES)
        nq_ref[0, p] = nq[:, sl].astype(BF16)
        nk_ref[0, p] = nk[:, sl].astype(BF16)
        nv_ref[0, p] = nv[:, sl].astype(BF16)


def _proj(x, wts, rope, tm, tk):
    B, S, _ = x.shape
    nt = S // tm
    per = tk // tm
    hp = NA_HEADS // 2
    head_out = lambda n: pl.BlockSpec((1, n, tm, LANES), lambda b, i: (b, 0, i, 0))
    names = ("g_mix_pre", "w_lat", "w_nq", "w_nk", "w_nv", "g_q_lat", "w_qt", "w_qrott", "g_kv_lat", "w_k", "w_vt")
    consts = [wts[n] for n in names]
    cos_tab, sin_tab, cos_t, sin_t = rope
    return pl.pallas_call(
        _proj_kernel,
        grid=(B, nt),
        in_specs=[pl.BlockSpec((1, tm, D_MODEL), lambda b, i: (b, i, 0))]
        + [_const_spec(c.shape) for c in consts]
        + [pl.BlockSpec((tm, LANES), lambda b, i: (i, 0))] * 2
        + [pl.BlockSpec((LANES, tm), lambda b, i: (0, i))] * 2,
        out_specs=[pl.BlockSpec((1, MLA_HEADS, 1, LANES, tm), lambda b, i: (b, 0, i, 0, 0)),
                   head_out(MLA_HEADS),
                   pl.BlockSpec((1, MLA_HEADS, 1, V_ROWS, tm), lambda b, i: (b, 0, i // per, 0, i % per)),
                   head_out(hp), head_out(hp), head_out(hp)],
        out_shape=[jax.ShapeDtypeStruct((B, MLA_HEADS, nt, LANES, tm), BF16),
                   jax.ShapeDtypeStruct((B, MLA_HEADS, S, LANES), BF16),
                   jax.ShapeDtypeStruct((B, MLA_HEADS, S // tk, V_ROWS, tk), BF16)]
        + [jax.ShapeDtypeStruct((B, hp, S, LANES), BF16)] * 3,
        compiler_params=pltpu.CompilerParams(
            dimension_semantics=("parallel", "parallel"), vmem_limit_bytes=VMEM_LIMIT),
        name="proj",
    )(x, *consts, cos_tab, sin_tab, cos_t, sin_t)


MLA_STRIP = 32


def _mla_kernel(qt_ref, k_ref, vt_ref, o_ref, s_ref, p_ref, acc_ref, mx_ref):
    nq, tq = qt_ref.shape[2], qt_ref.shape[4]
    nk, tk = vt_ref.shape[2], vt_ref.shape[4]
    assert nk % 2 == 0 and tk % MLA_STRIP == 0
    i = pl.program_id(2)

    def scores(qi, c, hd, slot):
        ks = pl.ds(pl.multiple_of(c * tk, tk), tk)
        st = _dot(k_ref[0, hd, ks, :], qt_ref[0, hd, qi])
        s_ref[slot, hd] = st
        mx = st[0:MLA_STRIP]
        for r in range(MLA_STRIP, tk, MLA_STRIP):
            mx = jnp.maximum(mx, st[r:r + MLA_STRIP])
        return jnp.max(mx, axis=0, keepdims=True)

    def numerators(hd, slot, m, mx):
        m_new = jnp.maximum(m, mx)
        alpha = jnp.exp2(m - m_new)
        for r in range(0, tk, MLA_STRIP):
            st = s_ref[slot, hd, r:r + MLA_STRIP, :]
            p_ref[slot, hd, r:r + MLA_STRIP, :] = jnp.exp2(st - m_new).astype(BF16)
        return m_new, alpha

    def accumulate(c, hd, slot, alpha):
        acc_ref[hd] = alpha * acc_ref[hd] + _dot(vt_ref[0, hd, c], p_ref[slot, hd])

    def chunk(c, cur, carry):
        nxt = 1 - cur
        wraps = c + 1 == nk
        qi_next = jnp.where(wraps, jnp.minimum(i + 1, nq - 1), i)
        c_next = jnp.where(wraps, 0, c + 1)
        out = []
        for hd in range(2):
            m, alpha_prev, mx = carry[hd]
            mx_next = scores(qi_next, c_next, hd, nxt)
            m_new, alpha = numerators(hd, cur, m, mx)
            accumulate(jnp.maximum(c - 1, 0), hd, nxt, alpha_prev)
            out.append((m_new, alpha, mx_next))
        return tuple(out)

    def step(c, carry):
        return lax.cond(c % 2 == 0, lambda: chunk(c, 0, carry), lambda: chunk(c, 1, carry))

    @pl.when(i == 0)
    def _():
        for hd in range(2):
            mx_ref[hd] = scores(0, 0, hd, 0)
            p_ref[1, hd] = jnp.zeros((tk, tq), BF16)
            acc_ref[hd] = jnp.zeros((V_ROWS, tq), F32)

    init = tuple((jnp.full((1, tq), -jnp.inf, F32), jnp.ones((1, tq), F32), mx_ref[hd]) for hd in range(2))
    fin = lax.fori_loop(0, nk, step, init)
    outs = []
    for hd in range(2):
        accumulate(nk - 1, hd, 1, fin[hd][1])
        mx_ref[hd] = fin[hd][2]
        acc = acc_ref[hd]
        outs.append(acc[:V_DIM] / acc[V_DIM:V_DIM + 1])
    o_ref[0] = jnp.concatenate(outs, axis=0).T.astype(o_ref.dtype)


def _mla(qt, k, vt):
    B, H, S, _ = k.shape
    nq, tq = qt.shape[2], qt.shape[4]
    nk, tk = vt.shape[2], vt.shape[4]
    once = dict(pipeline_mode=pl.Buffered(1))
    return pl.pallas_call(
        _mla_kernel,
        grid=(B, H // 2, nq),
        in_specs=[pl.BlockSpec((1, 2, nq, LANES, tq), lambda b, p, i: (b, p, 0, 0, 0), **once),
                  pl.BlockSpec((1, 2, S, LANES), lambda b, p, i: (b, p, 0, 0), **once),
                  pl.BlockSpec((1, 2, nk, V_ROWS, tk), lambda b, p, i: (b, p, 0, 0, 0), **once)],
        out_specs=pl.BlockSpec((1, tq, LANES), lambda b, p, i: (b, i, p)),
        out_shape=jax.ShapeDtypeStruct((B, S, MLA_WIDTH), BF16),
        scratch_shapes=[pltpu.VMEM((2, 2, tk, tq), F32), pltpu.VMEM((2, 2, tk, tq), BF16),
                        pltpu.VMEM((2, V_ROWS, tq), F32), pltpu.VMEM((2, 1, tq), F32)],
        compiler_params=pltpu.CompilerParams(
            dimension_semantics=("parallel", "parallel", "arbitrary"), vmem_limit_bytes=VMEM_LIMIT),
        name="mla_flash",
    )(qt, k, vt)


NA_QROWS = 4
NA_BROWS = NA_QROWS + NA_WIN_R
NA_QTOK = NA_QROWS * GRID_W
NA_BTOK = NA_BROWS * GRID_W


def _na_kernel(q_ref, k_ref, v_ref, tab_ref, o_ref, *, rows, nblk):
    i = pl.program_id(2)
    lane = lax.broadcasted_iota(jnp.int32, (NA_QTOK, LANES), 1)
    low = lane < NA_DIM

    def block(j, carry):
        r0 = (i * nblk + j) * NA_QROWS
        b0 = jnp.clip(r0 - NA_WIN_R // 2, 0, rows - NA_BROWS)
        var = jnp.where(r0 == 0, 0, jnp.where(r0 == rows - NA_QROWS, 2, 1))
        qs = pl.ds(pl.multiple_of(j * NA_QTOK, NA_QTOK), NA_QTOK)
        ks = pl.ds(pl.multiple_of(b0 * GRID_W, GRID_W), NA_BTOK)
        qp = q_ref[0, 0, qs, :]
        kb = k_ref[0, 0, ks, :]
        vb = v_ref[0, 0, ks, :]
        outs = []
        for hd in range(2):
            qh = jnp.where(low if hd == 0 else ~low, qp, jnp.zeros_like(qp))
            s = _dot_nt(qh, kb)
            t = tab_ref[var, hd]
            s = jnp.where(t == -jnp.inf, NEG_INF, s + t)
            m = jnp.max(s, axis=-1, keepdims=True)
            p = jnp.exp(s - m)
            l = jnp.sum(p, axis=-1, keepdims=True)
            outs.append(_dot(p.astype(BF16), vb) / l)
        o_ref[0, qs, :] = jnp.where(low, outs[0], outs[1]).astype(o_ref.dtype)
        return carry

    lax.fori_loop(0, nblk, block, 0)


def _natten(q, k, v, tab, nblk):
    B, HP, S, _ = q.shape
    rows = S // GRID_W
    assert rows >= NA_BROWS and rows % (NA_QROWS * nblk) == 0
    kern = functools.partial(_na_kernel, rows=rows, nblk=nblk)
    return pl.pallas_call(
        kern,
        grid=(HP, B, rows // (NA_QROWS * nblk)),
        in_specs=[pl.BlockSpec((1, 1, nblk * NA_QTOK, LANES), lambda p, b, i: (b, p, i, 0)),
                  pl.BlockSpec((1, 1, S, LANES), lambda p, b, i: (b, p, 0, 0)),
                  pl.BlockSpec((1, 1, S, LANES), lambda p, b, i: (b, p, 0, 0)),
                  pl.BlockSpec((3, 2, NA_QTOK, NA_BTOK), lambda p, b, i: (0, p, 0, 0))],
        out_specs=pl.BlockSpec((1, nblk * NA_QTOK, LANES), lambda p, b, i: (b, i, p)),
        out_shape=jax.ShapeDtypeStruct((B, S, NA_WIDTH), BF16),
        compiler_params=pltpu.CompilerParams(
            dimension_semantics=("parallel", "parallel", "arbitrary"), vmem_limit_bytes=VMEM_LIMIT),
        name="natten",
    )(q, k, v, tab)


def _oproj_kernel(a_ref, n_ref, x_ref, wa_ref, wn_ref, g_ref, o_ref):
    mix = _dot(a_ref[0], wa_ref[...]) + _dot(n_ref[0], wn_ref[...])
    o_ref[0] = x_ref[0] + _rms(mix, g_ref[...])


def _oproj(a, n, x, wts, tm):
    B, S, _ = x.shape
    tok = lambda w: pl.BlockSpec((1, tm, w), lambda b, i: (b, i, 0))
    consts = [wts["w_o_a"], wts["w_o_n"], wts["g_mix_post"]]
    return pl.pallas_call(
        _oproj_kernel,
        grid=(B, S // tm),
        in_specs=[tok(MLA_WIDTH), tok(NA_WIDTH), tok(D_MODEL)] + [_const_spec(c.shape) for c in consts],
        out_specs=tok(D_MODEL),
        out_shape=jax.ShapeDtypeStruct((B, S, D_MODEL), F32),
        compiler_params=pltpu.CompilerParams(
            dimension_semantics=("parallel", "parallel"), vmem_limit_bytes=VMEM_LIMIT),
        name="out_proj",
    )(a, n, x, *consts)


def _gelu_tanh(x):
    return 0.5 * x * (1.0 + jnp.tanh(0.7978845608028654 * (x + 0.044715 * (x * x * x))))


def _ffn_kernel(x_ref, prev_ref, next_ref, g_pre_ref, w_up_ref, cw_ref, cb_ref, w_dn_ref, g_post_ref, o_ref, *, fc):
    i = pl.program_id(1)
    last = pl.num_programs(1) - 1
    x = x_ref[0]
    tm = x.shape[0]
    prev = jnp.where(i > 0, prev_ref[0], 0.0)
    nxt = jnp.where(i < last, next_ref[0], 0.0)
    xh = jnp.concatenate([prev, x, nxt], axis=0)
    hn = _rms(xh, g_pre_ref[...]).astype(BF16)
    n_ext = tm + 2 * HALO

    def conv(hx, col):
        w = cw_ref[:, col:col + fc]
        left = pltpu.roll(hx, 1, 0)[HALO:HALO + tm]
        right = pltpu.roll(hx, n_ext - 1, 0)[HALO:HALO + tm]
        return cb_ref[:, col:col + fc] + left * w[0:1] + hx[HALO:HALO + tm] * w[1:2] + right * w[2:3]

    acc = jnp.zeros((tm, D_MODEL), F32)
    for c in range(D_FF // fc):
        cg = c * fc
        cu = D_FF + c * fc
        g = conv(_dot(hn, w_up_ref[:, cg:cg + fc]), cg)
        u = conv(_dot(hn, w_up_ref[:, cu:cu + fc]), cu)
        act = (_gelu_tanh(g) * u).astype(BF16)
        acc = acc + _dot(act, w_dn_ref[cg:cg + fc, :])
    o_ref[0] = x + _rms(acc, g_post_ref[...])


def _ffn(x, wts, tm, fc):
    B, S, _ = x.shape
    nb = tm // HALO
    kern = functools.partial(_ffn_kernel, fc=fc)
    consts = [wts["g_ffn_pre"], wts["w_ffn_up"], wts["ffn_conv_w"], wts["ffn_conv_b"], wts["w_ffn_down"],
              wts["g_ffn_post"]]
    return pl.pallas_call(
        kern,
        grid=(B, S // tm),
        in_specs=[pl.BlockSpec((1, tm, D_MODEL), lambda b, i: (b, i, 0)),
                  pl.BlockSpec((1, HALO, D_MODEL), lambda b, i: (b, jnp.maximum(i * nb - 1, 0), 0)),
                  pl.BlockSpec((1, HALO, D_MODEL), lambda b, i: (b, jnp.minimum((i + 1) * nb, S // HALO - 1), 0))]
        + [pl.BlockSpec(c.shape, lambda b, i, nd=c.ndim: (0,) * nd, pipeline_mode=pl.Buffered(1)) for c in consts],
        out_specs=pl.BlockSpec((1, tm, D_MODEL), lambda b, i: (b, i, 0)),
        out_shape=jax.ShapeDtypeStruct((B, S, D_MODEL), F32),
        compiler_params=pltpu.CompilerParams(
            dimension_semantics=("parallel", "parallel"), vmem_limit_bytes=VMEM_LIMIT),
        name="conv_ffn",
    )(x, x, x, *consts)


def _prep_weights(g_mix_pre, w_in, g_q_lat, w_q_up, g_kv_lat, w_kv_up, w_o, g_mix_post,
                  g_ffn_pre, w_ffn_up, ffn_conv_w, ffn_conv_b, w_ffn_down, g_ffn_post):
    half = QK_ROPE // 2
    o1, o2, o3 = Q_LORA, Q_LORA + KV_LORA, Q_LORA + KV_LORA + QK_ROPE
    row = lambda g: g.reshape(1, -1).astype(F32)

    w_kr = w_in[:, o2:o3]
    zl = jnp.zeros((D_MODEL, QK_NOPE), F32)
    zr = jnp.zeros((D_MODEL, LANES - QK_NOPE - QK_ROPE), F32)
    w_kr_blk = jnp.concatenate([zl, w_kr, zr], axis=1)
    w_krrot_blk = jnp.concatenate([zl, -w_kr[:, half:], w_kr[:, :half], zr], axis=1)
    w_lat = jnp.concatenate([w_in[:, :o2], w_kr_blk, w_krrot_blk], axis=1)

    wq = w_q_up.reshape(Q_LORA, MLA_HEADS, QK_NOPE + QK_ROPE)
    zq = jnp.zeros((Q_LORA, MLA_HEADS, LANES - QK_NOPE - QK_ROPE), F32)
    w_q = jnp.concatenate([wq, zq], axis=2).reshape(Q_LORA, MLA_HEADS * LANES)
    w_qrot = jnp.concatenate([jnp.zeros((Q_LORA, MLA_HEADS, QK_NOPE), F32), -wq[:, :, QK_NOPE + half:],
                              wq[:, :, QK_NOPE:QK_NOPE + half], zq], axis=2).reshape(Q_LORA, MLA_HEADS * LANES)

    wkv = w_kv_up.reshape(KV_LORA, MLA_HEADS, QK_NOPE + V_DIM)
    w_k = jnp.concatenate([wkv[:, :, :QK_NOPE], jnp.zeros((KV_LORA, MLA_HEADS, LANES - QK_NOPE), F32)],
                          axis=2).reshape(KV_LORA, MLA_HEADS * LANES)
    w_v = jnp.concatenate([wkv[:, :, QK_NOPE:], jnp.zeros((KV_LORA, MLA_HEADS, V_ROWS - V_DIM), F32)],
                          axis=2).reshape(KV_LORA, MLA_HEADS * V_ROWS)

    return {
        "g_mix_pre": row(g_mix_pre), "w_lat": w_lat.astype(BF16),
        "w_nq": w_in[:, o3:o3 + NA_WIDTH].astype(BF16),
        "w_nk": w_in[:, o3 + NA_WIDTH:o3 + 2 * NA_WIDTH].astype(BF16),
        "w_nv": w_in[:, o3 + 2 * NA_WIDTH:].astype(BF16),
        "g_q_lat": row(g_q_lat), "w_qt": w_q.T.astype(BF16), "w_qrott": w_qrot.T.astype(BF16),
        "g_kv_lat": row(g_kv_lat), "w_k": w_k.astype(BF16), "w_vt": w_v.T.astype(BF16),
        "w_o_a": w_o[:MLA_WIDTH].astype(BF16), "w_o_n": w_o[MLA_WIDTH:].astype(BF16), "g_mix_post": row(g_mix_post),
        "g_ffn_pre": row(g_ffn_pre), "w_ffn_up": w_ffn_up.astype(BF16), "ffn_conv_w": ffn_conv_w.astype(F32),
        "ffn_conv_b": row(ffn_conv_b), "w_ffn_down": w_ffn_down.astype(BF16), "g_ffn_post": row(g_ffn_post),
    }


def _rope_tables(S):
    inv = 1.0 / (ROPE_THETA ** (jnp.arange(0, QK_ROPE, 2, dtype=F32) / QK_ROPE))
    ang = jnp.arange(S, dtype=F32)[:, None] * inv[None, :]
    cos, sin = jnp.cos(ang), jnp.sin(ang)
    pad = jnp.zeros((S, LANES - QK_NOPE - QK_ROPE), F32)
    cos_tab = jnp.concatenate([jnp.ones((S, QK_NOPE), F32), cos, cos, pad], axis=1)
    sin_tab = jnp.concatenate([jnp.zeros((S, QK_NOPE), F32), sin, sin, pad], axis=1)
    return cos_tab, sin_tab, cos_tab.T, sin_tab.T


def _na_tables(rpb):
    qc = jnp.arange(GRID_W)[:, None]
    kc = jnp.arange(GRID_W)[None, :]
    dc = jnp.clip(kc - qc + (NA_WIN_C - 1), 0, 2 * NA_WIN_C - 2)
    qs = jnp.clip(qc - NA_WIN_C // 2, 0, GRID_W - NA_WIN_C)
    col_valid = (kc >= qs) & (kc < qs + NA_WIN_C)
    rpb = rpb.astype(F32)
    tile = jnp.zeros(rpb.shape[:2] + (GRID_W, GRID_W), F32)
    for e in range(2 * NA_WIN_C - 1):
        tile = jnp.where(dc == e, rpb[:, :, e][:, :, None, None], tile)
    tile = jnp.where(col_valid, tile, -jnp.inf)
    outside = jnp.full((NA_HEADS, GRID_W, GRID_W), -jnp.inf, F32)
    variants = []
    for shift, lo in ((0, lambda j: 0), (-NA_WIN_R // 2, lambda j: j), (-NA_WIN_R, lambda j: NA_BROWS - NA_WIN_R)):
        qrows = []
        for j in range(NA_QROWS):
            blocks = [tile[:, i - j + shift + NA_WIN_R - 1] if lo(j) <= i < lo(j) + NA_WIN_R else outside
                      for i in range(NA_BROWS)]
            qrows.append(jnp.concatenate(blocks, axis=2))
        variants.append(jnp.concatenate(qrows, axis=1))
    return jnp.stack(variants)


def _layer(x, wts, na_tab):
    B, S, _ = x.shape
    tm = 512
    qt, k, vt, nq, nk, nv = _proj(x, wts, _rope_tables(S), tm, tk=min(2048, S // 2))
    a = _mla(qt, k, vt)
    n = _natten(nq, nk, nv, na_tab, nblk=min(8, S // NA_QTOK))
    x1 = _oproj(a, n, x, wts, tm)
    return _ffn(x1, wts, tm, fc=256)


def kernel(x_prompt, x_sample, g_mix_pre, w_in, g_q_lat, w_q_up, g_kv_lat, w_kv_up, na_rpb, w_o, g_mix_post,
           g_ffn_pre, w_ffn_up, ffn_conv_w, ffn_conv_b, w_ffn_down, g_ffn_post):
    y_prompt, y_sample = x_prompt, x_sample
    for l in range(g_mix_pre.shape[0]):
        wts = _prep_weights(g_mix_pre[l], w_in[l], g_q_lat[l], w_q_up[l], g_kv_lat[l], w_kv_up[l], w_o[l],
                            g_mix_post[l], g_ffn_pre[l], w_ffn_up[l], ffn_conv_w[l], ffn_conv_b[l],
                            w_ffn_down[l], g_ffn_post[l])
        na_tab = _na_tables(na_rpb[l])
        y_prompt = _layer(y_prompt, wts, na_tab)
        y_sample = _layer(y_sample, wts, na_tab)
    return (y_prompt, y_sample)
```

```python
import functools

import jax
import jax.numpy as jnp
from jax import lax
from jax.experimental import pallas as pl
from jax.experimental.pallas import tpu as pltpu

D_MODEL = 1024
GRID_W = 64
MLA_HEADS = 8
Q_LORA = 256
KV_LORA = 128
QK_NOPE = 64
QK_ROPE = 32
V_DIM = 64
ROPE_THETA = 10000.0
NA_HEADS = 8
NA_DIM = 64
NA_WIN_R = 8
NA_WIN_C = 16
NA_WIDTH = NA_HEADS * NA_DIM
MLA_WIDTH = MLA_HEADS * V_DIM
D_FF = 2816
EPS = 1e-6
NEG_INF = -1e30

LANES = 128
V_ROWS = 80
LOG2_E = 1.4426950408889634
Q_SCALE = (QK_NOPE + QK_ROPE) ** -0.5 * LOG2_E
NA_Q_SCALE = NA_DIM ** -0.5 * LOG2_E
HALO = 8
VMEM_LIMIT = 56 * 1024 * 1024

BF16 = jnp.bfloat16
F32 = jnp.float32


def _rms(x, g):
    return x * lax.rsqrt(jnp.mean(x * x, axis=-1, keepdims=True) + EPS) * g


def _dot(a, b):
    return jnp.dot(a, b, preferred_element_type=F32)


def _const_spec(shape):
    return pl.BlockSpec(shape, lambda *_: (0,) * len(shape))


def _proj_kernel(x_ref, g_pre_ref, w_lat_ref, w_nqt_ref, w_nk_ref, w_nvt_ref, g_q_ref, w_qt_ref, w_qrott_ref,
                 g_kv_ref, w_k_ref, w_vt_ref, cos_ref, sin_ref, cost_ref, sint_ref,
                 qt_ref, k_ref, vt_ref, nqt_ref, nk_ref, nvt_ref):
    hn = _rms(x_ref[0], g_pre_ref[...])
    h = hn.astype(BF16)
    tm = h.shape[0]
    cos = cos_ref[...]
    sin = sin_ref[...]
    cost = cost_ref[...]
    sint = sint_ref[...]

    z = _dot(h, w_lat_ref[...])
    c_q = z[:, :Q_LORA]
    c_kv = z[:, Q_LORA:Q_LORA + KV_LORA]
    k_r = z[:, Q_LORA + KV_LORA:Q_LORA + KV_LORA + LANES]
    k_rrot = z[:, Q_LORA + KV_LORA + LANES:]
    k_pe = k_r * cos + k_rrot * sin

    cqn = _rms(c_q, g_q_ref[...])
    cqnt = cqn.T.astype(BF16)
    qmt = _dot(w_qt_ref[...], cqnt)
    qrt = _dot(w_qrott_ref[...], cqnt)
    ckvn = _rms(c_kv, g_kv_ref[...])
    kn = _dot(ckvn.astype(BF16), w_k_ref[...])
    vvt = _dot(w_vt_ref[...], ckvn.T.astype(BF16))
    ones_row = lax.broadcasted_iota(jnp.int32, (V_ROWS, tm), 0) == V_DIM
    for hd in range(MLA_HEADS):
        sl = slice(hd * LANES, (hd + 1) * LANES)
        qt_ref[0, hd, 0] = ((qmt[sl] * cost + qrt[sl] * sint) * Q_SCALE).astype(BF16)
        k_ref[0, hd] = (kn[:, sl] + k_pe).astype(BF16)
        vt_ref[0, hd, 0] = jnp.where(ones_row, 1.0, vvt[hd * V_ROWS:(hd + 1) * V_ROWS]).astype(BF16)

    ht = hn.T.astype(BF16)
    nk = _dot(h, w_nk_ref[...])
    nqt = _dot(w_nqt_ref[...], ht) * NA_Q_SCALE
    nvt = _dot(w_nvt_ref[...], ht)
    low_rows = lax.broadcasted_iota(jnp.int32, (LANES, NA_QTOK), 0) < NA_DIM
    na_ones_row = lax.broadcasted_iota(jnp.int32, (V_ROWS, NA_QTOK), 0) == NA_DIM
    for p in range(NA_HEADS // 2):
        nk_ref[0, p] = nk[:, p * LANES:(p + 1) * LANES].astype(BF16)
        for t in range(tm // NA_QTOK):
            cols = slice(t * NA_QTOK, (t + 1) * NA_QTOK)
            pair = nqt[p * LANES:(p + 1) * LANES, cols]
            nqt_ref[0, 2 * p, t] = jnp.where(low_rows, pair, 0.0).astype(BF16)
            nqt_ref[0, 2 * p + 1, t] = jnp.where(low_rows, 0.0, pair).astype(BF16)
            for hd in (2 * p, 2 * p + 1):
                nvt_ref[0, hd, t] = jnp.where(na_ones_row, 1.0, nvt[hd * V_ROWS:(hd + 1) * V_ROWS, cols]).astype(BF16)


def _proj(x, wts, rope, tm, tk):
    B, S, _ = x.shape
    nt = S // tm
    per = tk // tm
    nqt = tm // NA_QTOK
    hp = NA_HEADS // 2
    head_out = lambda n: pl.BlockSpec((1, n, tm, LANES), lambda b, i: (b, 0, i, 0))
    names = ("g_mix_pre", "w_lat", "w_nqt", "w_nk", "w_nvt", "g_q_lat", "w_qt", "w_qrott", "g_kv_lat", "w_k", "w_vt")
    consts = [wts[n] for n in names]
    cos_tab, sin_tab, cos_t, sin_t = rope
    return pl.pallas_call(
        _proj_kernel,
        grid=(B, nt),
        in_specs=[pl.BlockSpec((1, tm, D_MODEL), lambda b, i: (b, i, 0))]
        + [_const_spec(c.shape) for c in consts]
        + [pl.BlockSpec((tm, LANES), lambda b, i: (i, 0))] * 2
        + [pl.BlockSpec((LANES, tm), lambda b, i: (0, i))] * 2,
        out_specs=[pl.BlockSpec((1, MLA_HEADS, 1, LANES, tm), lambda b, i: (b, 0, i, 0, 0)),
                   head_out(MLA_HEADS),
                   pl.BlockSpec((1, MLA_HEADS, 1, V_ROWS, tm), lambda b, i: (b, 0, i // per, 0, i % per)),
                   pl.BlockSpec((1, NA_HEADS, nqt, LANES, NA_QTOK), lambda b, i: (b, 0, i, 0, 0)),
                   head_out(hp),
                   pl.BlockSpec((1, NA_HEADS, nqt, V_ROWS, NA_QTOK), lambda b, i: (b, 0, i, 0, 0))],
        out_shape=[jax.ShapeDtypeStruct((B, MLA_HEADS, nt, LANES, tm), BF16),
                   jax.ShapeDtypeStruct((B, MLA_HEADS, S, LANES), BF16),
                   jax.ShapeDtypeStruct((B, MLA_HEADS, S // tk, V_ROWS, tk), BF16),
                   jax.ShapeDtypeStruct((B, NA_HEADS, S // NA_QTOK, LANES, NA_QTOK), BF16),
                   jax.ShapeDtypeStruct((B, hp, S, LANES), BF16),
                   jax.ShapeDtypeStruct((B, NA_HEADS, S // NA_QTOK, V_ROWS, NA_QTOK), BF16)],
        compiler_params=pltpu.CompilerParams(
            dimension_semantics=("parallel", "parallel"), vmem_limit_bytes=VMEM_LIMIT),
        name="proj",
    )(x, *consts, cos_tab, sin_tab, cos_t, sin_t)


MLA_STRIP = 32


def _mla_kernel(qt_ref, k_ref, vt_ref, o_ref, s_ref, p_ref, acc_ref, mx_ref):
    nq, tq = qt_ref.shape[2], qt_ref.shape[4]
    nk, tk = vt_ref.shape[2], vt_ref.shape[4]
    assert nk % 2 == 0 and tk % MLA_STRIP == 0
    i = pl.program_id(2)

    def scores(qi, c, hd, slot):
        ks = pl.ds(pl.multiple_of(c * tk, tk), tk)
        st = _dot(k_ref[0, hd, ks, :], qt_ref[0, hd, qi])
        s_ref[slot, hd] = st
        mx = st[0:MLA_STRIP]
        for r in range(MLA_STRIP, tk, MLA_STRIP):
            mx = jnp.maximum(mx, st[r:r + MLA_STRIP])
        return jnp.max(mx, axis=0, keepdims=True)

    def numerators(hd, slot, m, mx):
        m_new = jnp.maximum(m, mx)
        alpha = jnp.exp2(m - m_new)
        for r in range(0, tk, MLA_STRIP):
            st = s_ref[slot, hd, r:r + MLA_STRIP, :]
            p_ref[slot, hd, r:r + MLA_STRIP, :] = jnp.exp2(st - m_new).astype(BF16)
        return m_new, alpha

    def accumulate(c, hd, slot, alpha):
        acc_ref[hd] = alpha * acc_ref[hd] + _dot(vt_ref[0, hd, c], p_ref[slot, hd])

    def chunk(c, cur, carry):
        nxt = 1 - cur
        wraps = c + 1 == nk
        qi_next = jnp.where(wraps, jnp.minimum(i + 1, nq - 1), i)
        c_next = jnp.where(wraps, 0, c + 1)
        out = []
        for hd in range(2):
            m, alpha_prev, mx = carry[hd]
            mx_next = scores(qi_next, c_next, hd, nxt)
            m_new, alpha = numerators(hd, cur, m, mx)
            accumulate(jnp.maximum(c - 1, 0), hd, nxt, alpha_prev)
            out.append((m_new, alpha, mx_next))
        return tuple(out)

    def step(c, carry):
        return lax.cond(c % 2 == 0, lambda: chunk(c, 0, carry), lambda: chunk(c, 1, carry))

    @pl.when(i == 0)
    def _():
        for hd in range(2):
            mx_ref[hd] = scores(0, 0, hd, 0)
            p_ref[1, hd] = jnp.zeros((tk, tq), BF16)
            acc_ref[hd] = jnp.zeros((V_ROWS, tq), F32)

    init = tuple((jnp.full((1, tq), -jnp.inf, F32), jnp.ones((1, tq), F32), mx_ref[hd]) for hd in range(2))
    fin = lax.fori_loop(0, nk, step, init)
    outs = []
    for hd in range(2):
        accumulate(nk - 1, hd, 1, fin[hd][1])
        mx_ref[hd] = fin[hd][2]
        acc = acc_ref[hd]
        outs.append(acc[:V_DIM] / acc[V_DIM:V_DIM + 1])
    o_ref[0] = jnp.concatenate(outs, axis=0).T.astype(o_ref.dtype)


def _mla(qt, k, vt):
    B, H, S, _ = k.shape
    nq, tq = qt.shape[2], qt.shape[4]
    nk, tk = vt.shape[2], vt.shape[4]
    once = dict(pipeline_mode=pl.Buffered(1))
    return pl.pallas_call(
        _mla_kernel,
        grid=(B, H // 2, nq),
        in_specs=[pl.BlockSpec((1, 2, nq, LANES, tq), lambda b, p, i: (b, p, 0, 0, 0), **once),
                  pl.BlockSpec((1, 2, S, LANES), lambda b, p, i: (b, p, 0, 0), **once),
                  pl.BlockSpec((1, 2, nk, V_ROWS, tk), lambda b, p, i: (b, p, 0, 0, 0), **once)],
        out_specs=pl.BlockSpec((1, tq, LANES), lambda b, p, i: (b, i, p)),
        out_shape=jax.ShapeDtypeStruct((B, S, MLA_WIDTH), BF16),
        scratch_shapes=[pltpu.VMEM((2, 2, tk, tq), F32), pltpu.VMEM((2, 2, tk, tq), BF16),
                        pltpu.VMEM((2, V_ROWS, tq), F32), pltpu.VMEM((2, 1, tq), F32)],
        compiler_params=pltpu.CompilerParams(
            dimension_semantics=("parallel", "parallel", "arbitrary"), vmem_limit_bytes=VMEM_LIMIT),
        name="mla_flash",
    )(qt, k, vt)


NA_QROWS = 4
NA_BROWS = NA_QROWS + NA_WIN_R
NA_QTOK = NA_QROWS * GRID_W
NA_BTOK = NA_BROWS * GRID_W
NA_STRIP = 32
NA_TILES = NA_BROWS // NA_QROWS


def _na_kernel(qt_ref, k_ref, vt_ref, tab_ref, o_ref, s_ref, p_ref, *, nblocks, nblk):
    i = pl.program_id(2)

    def band(j):
        g = i * nblk + j
        t0 = jnp.clip(g - NA_WIN_R // 2 // NA_QROWS, 0, nblocks - NA_TILES)
        var = jnp.where(g == 0, 0, jnp.where(g == nblocks - 1, 2, 1))
        return t0, var

    def scores(j, slot):
        t0, var = band(j)
        kb = k_ref[0, 0, pl.ds(pl.multiple_of(t0 * NA_QTOK, NA_QTOK), NA_BTOK), :]
        mxs = []
        for hd in range(2):
            st = _dot(kb, qt_ref[0, hd, j]) + tab_ref[var, hd]
            s_ref[slot, hd] = st
            mx = st[0:NA_STRIP]
            for r in range(NA_STRIP, NA_BTOK, NA_STRIP):
                mx = jnp.maximum(mx, st[r:r + NA_STRIP])
            mxs.append(jnp.max(mx, axis=0, keepdims=True))
        return tuple(mxs)

    def numerators(slot, mxs):
        for hd in range(2):
            for r in range(0, NA_BTOK, NA_STRIP):
                st = s_ref[slot, hd, r:r + NA_STRIP, :]
                p_ref[slot, hd, r:r + NA_STRIP, :] = jnp.exp2(st - mxs[hd]).astype(BF16)

    def values(j, slot):
        t0, _ = band(j)
        outs = []
        for hd in range(2):
            acc = _dot(vt_ref[0, hd, t0], p_ref[slot, hd, 0:NA_QTOK, :])
            for t in range(1, NA_TILES):
                acc = acc + _dot(vt_ref[0, hd, t0 + t], p_ref[slot, hd, t * NA_QTOK:(t + 1) * NA_QTOK, :])
            outs.append(acc[:NA_DIM] / acc[NA_DIM:NA_DIM + 1])
        rows = pl.ds(pl.multiple_of(j * NA_QTOK, NA_QTOK), NA_QTOK)
        o_ref[0, rows, :] = jnp.concatenate(outs, axis=0).T.astype(o_ref.dtype)

    def block(j, cur, mxs):
        mxs_next = scores(jnp.minimum(j + 1, nblk - 1), 1 - cur)
        numerators(cur, mxs)
        values(jnp.maximum(j - 1, 0), 1 - cur)
        return mxs_next

    def step(j, mxs):
        return lax.cond(j % 2 == 0, lambda: block(j, 0, mxs), lambda: block(j, 1, mxs))

    mxs0 = scores(0, 0)
    p_ref[1] = jnp.ones(p_ref.shape[1:], BF16)
    lax.fori_loop(0, nblk, step, mxs0)
    values(nblk - 1, (nblk - 1) % 2)


def _natten(qt, k, vt, tab, nblk):
    B, HP, S, _ = k.shape
    nblocks = S // NA_QTOK
    assert nblocks >= NA_TILES and nblocks % nblk == 0 and nblk >= 2
    kern = functools.partial(_na_kernel, nblocks=nblocks, nblk=nblk)
    return pl.pallas_call(
        kern,
        grid=(HP, B, nblocks // nblk),
        in_specs=[pl.BlockSpec((1, 2, nblk, LANES, NA_QTOK), lambda p, b, i: (b, p, i, 0, 0)),
                  pl.BlockSpec((1, 1, S, LANES), lambda p, b, i: (b, p, 0, 0)),
                  pl.BlockSpec((1, 2, nblocks, V_ROWS, NA_QTOK), lambda p, b, i: (b, p, 0, 0, 0)),
                  pl.BlockSpec((3, 2, NA_BTOK, NA_QTOK), lambda p, b, i: (0, p, 0, 0))],
        out_specs=pl.BlockSpec((1, nblk * NA_QTOK, LANES), lambda p, b, i: (b, i, p)),
        out_shape=jax.ShapeDtypeStruct((B, S, NA_WIDTH), BF16),
        scratch_shapes=[pltpu.VMEM((2, 2, NA_BTOK, NA_QTOK), F32), pltpu.VMEM((2, 2, NA_BTOK, NA_QTOK), BF16)],
        compiler_params=pltpu.CompilerParams(
            dimension_semantics=("parallel", "parallel", "arbitrary"), vmem_limit_bytes=VMEM_LIMIT),
        name="natten",
    )(qt, k, vt, tab)


def _oproj_kernel(a_ref, n_ref, x_ref, wa_ref, wn_ref, g_ref, o_ref):
    mix = _dot(a_ref[0], wa_ref[...]) + _dot(n_ref[0], wn_ref[...])
    o_ref[0] = x_ref[0] + _rms(mix, g_ref[...])


def _oproj(a, n, x, wts, tm):
    B, S, _ = x.shape
    tok = lambda w: pl.BlockSpec((1, tm, w), lambda b, i: (b, i, 0))
    consts = [wts["w_o_a"], wts["w_o_n"], wts["g_mix_post"]]
    return pl.pallas_call(
        _oproj_kernel,
        grid=(B, S // tm),
        in_specs=[tok(MLA_WIDTH), tok(NA_WIDTH), tok(D_MODEL)] + [_const_spec(c.shape) for c in consts],
        out_specs=tok(D_MODEL),
        out_shape=jax.ShapeDtypeStruct((B, S, D_MODEL), F32),
        compiler_params=pltpu.CompilerParams(
            dimension_semantics=("parallel", "parallel"), vmem_limit_bytes=VMEM_LIMIT),
        name="out_proj",
    )(a, n, x, *consts)


def _gelu_tanh(x):
    return 0.5 * x * (1.0 + jnp.tanh(0.7978845608028654 * (x + 0.044715 * (x * x * x))))


def _ffn_kernel(x_ref, prev_ref, next_ref, g_pre_ref, w_up_ref, cw_ref, cb_ref, w_dn_ref, g_post_ref, o_ref, *, fc):
    i = pl.program_id(1)
    last = pl.num_programs(1) - 1
    x = x_ref[0]
    tm = x.shape[0]
    prev = jnp.where(i > 0, prev_ref[0], 0.0)
    nxt = jnp.where(i < last, next_ref[0], 0.0)
    xh = jnp.concatenate([prev, x, nxt], axis=0)
    hn = _rms(xh, g_pre_ref[...]).astype(BF16)
    n_ext = tm + 2 * HALO

    def conv(hx, col):
        w = cw_ref[:, col:col + fc]
        left = pltpu.roll(hx, 1, 0)[HALO:HALO + tm]
        right = pltpu.roll(hx, n_ext - 1, 0)[HALO:HALO + tm]
        return cb_ref[:, col:col + fc] + left * w[0:1] + hx[HALO:HALO + tm] * w[1:2] + right * w[2:3]

    acc = jnp.zeros((tm, D_MODEL), F32)
    for c in range(D_FF // fc):
        cg = c * fc
        cu = D_FF + c * fc
        g = conv(_dot(hn, w_up_ref[:, cg:cg + fc]), cg)
        u = conv(_dot(hn, w_up_ref[:, cu:cu + fc]), cu)
        act = (_gelu_tanh(g) * u).astype(BF16)
        acc = acc + _dot(act, w_dn_ref[cg:cg + fc, :])
    o_ref[0] = x + _rms(acc, g_post_ref[...])


def _ffn(x, wts, tm, fc):
    B, S, _ = x.shape
    nb = tm // HALO
    kern = functools.partial(_ffn_kernel, fc=fc)
    consts = [wts["g_ffn_pre"], wts["w_ffn_up"], wts["ffn_conv_w"], wts["ffn_conv_b"], wts["w_ffn_down"],
              wts["g_ffn_post"]]
    return pl.pallas_call(
        kern,
        grid=(B, S // tm),
        in_specs=[pl.BlockSpec((1, tm, D_MODEL), lambda b, i: (b, i, 0)),
                  pl.BlockSpec((1, HALO, D_MODEL), lambda b, i: (b, jnp.maximum(i * nb - 1, 0), 0)),
                  pl.BlockSpec((1, HALO, D_MODEL), lambda b, i: (b, jnp.minimum((i + 1) * nb, S // HALO - 1), 0))]
        + [pl.BlockSpec(c.shape, lambda b, i, nd=c.ndim: (0,) * nd, pipeline_mode=pl.Buffered(1)) for c in consts],
        out_specs=pl.BlockSpec((1, tm, D_MODEL), lambda b, i: (b, i, 0)),
        out_shape=jax.ShapeDtypeStruct((B, S, D_MODEL), F32),
        compiler_params=pltpu.CompilerParams(
            dimension_semantics=("parallel", "parallel"), vmem_limit_bytes=VMEM_LIMIT),
        name="conv_ffn",
    )(x, x, x, *consts)


def _prep_weights(g_mix_pre, w_in, g_q_lat, w_q_up, g_kv_lat, w_kv_up, w_o, g_mix_post,
                  g_ffn_pre, w_ffn_up, ffn_conv_w, ffn_conv_b, w_ffn_down, g_ffn_post):
    half = QK_ROPE // 2
    o1, o2, o3 = Q_LORA, Q_LORA + KV_LORA, Q_LORA + KV_LORA + QK_ROPE
    row = lambda g: g.reshape(1, -1).astype(F32)

    w_kr = w_in[:, o2:o3]
    zl = jnp.zeros((D_MODEL, QK_NOPE), F32)
    zr = jnp.zeros((D_MODEL, LANES - QK_NOPE - QK_ROPE), F32)
    w_kr_blk = jnp.concatenate([zl, w_kr, zr], axis=1)
    w_krrot_blk = jnp.concatenate([zl, -w_kr[:, half:], w_kr[:, :half], zr], axis=1)
    w_lat = jnp.concatenate([w_in[:, :o2], w_kr_blk, w_krrot_blk], axis=1)

    wq = w_q_up.reshape(Q_LORA, MLA_HEADS, QK_NOPE + QK_ROPE)
    zq = jnp.zeros((Q_LORA, MLA_HEADS, LANES - QK_NOPE - QK_ROPE), F32)
    w_q = jnp.concatenate([wq, zq], axis=2).reshape(Q_LORA, MLA_HEADS * LANES)
    w_qrot = jnp.concatenate([jnp.zeros((Q_LORA, MLA_HEADS, QK_NOPE), F32), -wq[:, :, QK_NOPE + half:],
                              wq[:, :, QK_NOPE:QK_NOPE + half], zq], axis=2).reshape(Q_LORA, MLA_HEADS * LANES)

    wkv = w_kv_up.reshape(KV_LORA, MLA_HEADS, QK_NOPE + V_DIM)
    w_k = jnp.concatenate([wkv[:, :, :QK_NOPE], jnp.zeros((KV_LORA, MLA_HEADS, LANES - QK_NOPE), F32)],
                          axis=2).reshape(KV_LORA, MLA_HEADS * LANES)
    w_v = jnp.concatenate([wkv[:, :, QK_NOPE:], jnp.zeros((KV_LORA, MLA_HEADS, V_ROWS - V_DIM), F32)],
                          axis=2).reshape(KV_LORA, MLA_HEADS * V_ROWS)

    w_nv = jnp.concatenate([w_in[:, o3 + 2 * NA_WIDTH:].reshape(D_MODEL, NA_HEADS, NA_DIM),
                            jnp.zeros((D_MODEL, NA_HEADS, V_ROWS - NA_DIM), F32)],
                           axis=2).reshape(D_MODEL, NA_HEADS * V_ROWS)

    return {
        "g_mix_pre": row(g_mix_pre), "w_lat": w_lat.astype(BF16),
        "w_nqt": w_in[:, o3:o3 + NA_WIDTH].T.astype(BF16),
        "w_nk": w_in[:, o3 + NA_WIDTH:o3 + 2 * NA_WIDTH].astype(BF16),
        "w_nvt": w_nv.T.astype(BF16),
        "g_q_lat": row(g_q_lat), "w_qt": w_q.T.astype(BF16), "w_qrott": w_qrot.T.astype(BF16),
        "g_kv_lat": row(g_kv_lat), "w_k": w_k.astype(BF16), "w_vt": w_v.T.astype(BF16),
        "w_o_a": w_o[:MLA_WIDTH].astype(BF16), "w_o_n": w_o[MLA_WIDTH:].astype(BF16), "g_mix_post": row(g_mix_post),
        "g_ffn_pre": row(g_ffn_pre), "w_ffn_up": w_ffn_up.astype(BF16), "ffn_conv_w": ffn_conv_w.astype(F32),
        "ffn_conv_b": row(ffn_conv_b), "w_ffn_down": w_ffn_down.astype(BF16), "g_ffn_post": row(g_ffn_post),
    }


def _rope_tables(S):
    inv = 1.0 / (ROPE_THETA ** (jnp.arange(0, QK_ROPE, 2, dtype=F32) / QK_ROPE))
    ang = jnp.arange(S, dtype=F32)[:, None] * inv[None, :]
    cos, sin = jnp.cos(ang), jnp.sin(ang)
    pad = jnp.zeros((S, LANES - QK_NOPE - QK_ROPE), F32)
    cos_tab = jnp.concatenate([jnp.ones((S, QK_NOPE), F32), cos, cos, pad], axis=1)
    sin_tab = jnp.concatenate([jnp.zeros((S, QK_NOPE), F32), sin, sin, pad], axis=1)
    return cos_tab, sin_tab, cos_tab.T, sin_tab.T


def _na_tables(rpb):
    qc = jnp.arange(GRID_W)[:, None]
    kc = jnp.arange(GRID_W)[None, :]
    dc = jnp.clip(kc - qc + (NA_WIN_C - 1), 0, 2 * NA_WIN_C - 2)
    qs = jnp.clip(qc - NA_WIN_C // 2, 0, GRID_W - NA_WIN_C)
    col_valid = (kc >= qs) & (kc < qs + NA_WIN_C)
    rpb = rpb.astype(F32)
    tile = jnp.zeros(rpb.shape[:2] + (GRID_W, GRID_W), F32)
    for e in range(2 * NA_WIN_C - 1):
        tile = jnp.where(dc == e, rpb[:, :, e][:, :, None, None], tile)
    tile = jnp.where(col_valid, tile * LOG2_E, NEG_INF)
    outside = jnp.full((NA_HEADS, GRID_W, GRID_W), NEG_INF, F32)
    variants = []
    for shift, lo in ((0, lambda j: 0), (-NA_WIN_R // 2, lambda j: j), (-NA_WIN_R, lambda j: NA_BROWS - NA_WIN_R)):
        qrows = []
        for j in range(NA_QROWS):
            blocks = [tile[:, i - j + shift + NA_WIN_R - 1] if lo(j) <= i < lo(j) + NA_WIN_R else outside
                      for i in range(NA_BROWS)]
            qrows.append(jnp.concatenate(blocks, axis=2))
        variants.append(jnp.concatenate(qrows, axis=1))
    return jnp.swapaxes(jnp.stack(variants), 2, 3)


def _layer(x, wts, na_tab):
    B, S, _ = x.shape
    tm = 512
    qt, k, vt, nqt, nk, nvt = _proj(x, wts, _rope_tables(S), tm, tk=min(2048, S // 2))
    a = _mla(qt, k, vt)
    n = _natten(nqt, nk, nvt, na_tab, nblk=min(16, S // NA_QTOK))
    x1 = _oproj(a, n, x, wts, tm)
    return _ffn(x1, wts, tm, fc=256)


def kernel(x_prompt, x_sample, g_mix_pre, w_in, g_q_lat, w_q_up, g_kv_lat, w_kv_up, na_rpb, w_o, g_mix_post,
           g_ffn_pre, w_ffn_up, ffn_conv_w, ffn_conv_b, w_ffn_down, g_ffn_post):
    y_prompt, y_sample = x_prompt, x_sample
    for l in range(g_mix_pre.shape[0]):
        wts = _prep_weights(g_mix_pre[l], w_in[l], g_q_lat[l], w_q_up[l], g_kv_lat[l], w_kv_up[l], w_o[l],
                            g_mix_post[l], g_ffn_pre[l], w_ffn_up[l], ffn_conv_w[l], ffn_conv_b[l],
                            w_ffn_down[l], g_ffn_post[l])
        na_tab = _na_tables(na_rpb[l])
        y_prompt = _layer(y_prompt, wts, na_tab)
        y_sample = _layer(y_sample, wts, na_tab)
    return (y_prompt, y_sample)
```

```python
import functools

import jax
import jax.numpy as jnp
from jax import lax
from jax.experimental import pallas as pl
from jax.experimental.pallas import tpu as pltpu

D_MODEL = 1024
GRID_W = 64
MLA_HEADS = 8
Q_LORA = 256
KV_LORA = 128
QK_NOPE = 64
QK_ROPE = 32
V_DIM = 64
ROPE_THETA = 10000.0
NA_HEADS = 8
NA_DIM = 64
NA_WIN_R = 8
NA_WIN_C = 16
NA_WIDTH = NA_HEADS * NA_DIM
MLA_WIDTH = MLA_HEADS * V_DIM
D_FF = 2816
EPS = 1e-6
NEG_INF = -1e30

LANES = 128
V_ROWS = 80
LOG2_E = 1.4426950408889634
Q_SCALE = (QK_NOPE + QK_ROPE) ** -0.5 * LOG2_E
NA_Q_SCALE = NA_DIM ** -0.5 * LOG2_E
HALO = 8
VMEM_LIMIT = 56 * 1024 * 1024

BF16 = jnp.bfloat16
F32 = jnp.float32


def _rms(x, g):
    return x * lax.rsqrt(jnp.mean(x * x, axis=-1, keepdims=True) + EPS) * g


def _dot(a, b):
    return jnp.dot(a, b, preferred_element_type=F32)


def _const_spec(shape):
    return pl.BlockSpec(shape, lambda *_: (0,) * len(shape))


def _proj_kernel(x_ref, g_pre_ref, w_lat_ref, w_nqt_ref, w_nk_ref, w_nvt_ref, g_q_ref, w_qt_ref, w_qrott_ref,
                 g_kv_ref, w_k_ref, w_vt_ref, cos_ref, sin_ref, cost_ref, sint_ref,
                 qt_ref, k_ref, vt_ref, nqt_ref, nk_ref, nvt_ref):
    hn = _rms(x_ref[0], g_pre_ref[...])
    h = hn.astype(BF16)
    tm = h.shape[0]
    cos = cos_ref[...]
    sin = sin_ref[...]
    cost = cost_ref[...]
    sint = sint_ref[...]

    z = _dot(h, w_lat_ref[...])
    c_q = z[:, :Q_LORA]
    c_kv = z[:, Q_LORA:Q_LORA + KV_LORA]
    k_r = z[:, Q_LORA + KV_LORA:Q_LORA + KV_LORA + LANES]
    k_rrot = z[:, Q_LORA + KV_LORA + LANES:]
    k_pe = k_r * cos + k_rrot * sin

    cqn = _rms(c_q, g_q_ref[...])
    cqnt = cqn.T.astype(BF16)
    qmt = _dot(w_qt_ref[...], cqnt)
    qrt = _dot(w_qrott_ref[...], cqnt)
    ckvn = _rms(c_kv, g_kv_ref[...])
    kn = _dot(ckvn.astype(BF16), w_k_ref[...])
    vvt = _dot(w_vt_ref[...], ckvn.T.astype(BF16))
    ones_row = lax.broadcasted_iota(jnp.int32, (V_ROWS, tm), 0) == V_DIM
    for hd in range(MLA_HEADS):
        sl = slice(hd * LANES, (hd + 1) * LANES)
        qt_ref[0, hd, 0] = ((qmt[sl] * cost + qrt[sl] * sint) * Q_SCALE).astype(BF16)
        k_ref[0, hd] = (kn[:, sl] + k_pe).astype(BF16)
        vt_ref[0, hd, 0] = jnp.where(ones_row, 1.0, vvt[hd * V_ROWS:(hd + 1) * V_ROWS]).astype(BF16)

    ht = hn.T.astype(BF16)
    nk = _dot(h, w_nk_ref[...])
    nqt = _dot(w_nqt_ref[...], ht) * NA_Q_SCALE
    nvt = _dot(w_nvt_ref[...], ht)
    low_rows = lax.broadcasted_iota(jnp.int32, (LANES, NA_QTOK), 0) < NA_DIM
    na_ones_row = lax.broadcasted_iota(jnp.int32, (V_ROWS, NA_QTOK), 0) == NA_DIM
    for p in range(NA_HEADS // 2):
        nk_ref[0, p] = nk[:, p * LANES:(p + 1) * LANES].astype(BF16)
        for t in range(tm // NA_QTOK):
            cols = slice(t * NA_QTOK, (t + 1) * NA_QTOK)
            pair = nqt[p * LANES:(p + 1) * LANES, cols]
            nqt_ref[0, 2 * p, t] = jnp.where(low_rows, pair, 0.0).astype(BF16)
            nqt_ref[0, 2 * p + 1, t] = jnp.where(low_rows, 0.0, pair).astype(BF16)
            for hd in (2 * p, 2 * p + 1):
                nvt_ref[0, hd, t] = jnp.where(na_ones_row, 1.0, nvt[hd * V_ROWS:(hd + 1) * V_ROWS, cols]).astype(BF16)


def _proj(x, wts, rope, tm, tk):
    B, S, _ = x.shape
    nt = S // tm
    per = tk // tm
    nqt = tm // NA_QTOK
    hp = NA_HEADS // 2
    head_out = lambda n: pl.BlockSpec((1, n, tm, LANES), lambda b, i: (b, 0, i, 0))
    names = ("g_mix_pre", "w_lat", "w_nqt", "w_nk", "w_nvt", "g_q_lat", "w_qt", "w_qrott", "g_kv_lat", "w_k", "w_vt")
    consts = [wts[n] for n in names]
    cos_tab, sin_tab, cos_t, sin_t = rope
    return pl.pallas_call(
        _proj_kernel,
        grid=(B, nt),
        in_specs=[pl.BlockSpec((1, tm, D_MODEL), lambda b, i: (b, i, 0))]
        + [_const_spec(c.shape) for c in consts]
        + [pl.BlockSpec((tm, LANES), lambda b, i: (i, 0))] * 2
        + [pl.BlockSpec((LANES, tm), lambda b, i: (0, i))] * 2,
        out_specs=[pl.BlockSpec((1, MLA_HEADS, 1, LANES, tm), lambda b, i: (b, 0, i, 0, 0)),
                   head_out(MLA_HEADS),
                   pl.BlockSpec((1, MLA_HEADS, 1, V_ROWS, tm), lambda b, i: (b, 0, i // per, 0, i % per)),
                   pl.BlockSpec((1, NA_HEADS, nqt, LANES, NA_QTOK), lambda b, i: (b, 0, i, 0, 0)),
                   head_out(hp),
                   pl.BlockSpec((1, NA_HEADS, nqt, V_ROWS, NA_QTOK), lambda b, i: (b, 0, i, 0, 0))],
        out_shape=[jax.ShapeDtypeStruct((B, MLA_HEADS, nt, LANES, tm), BF16),
                   jax.ShapeDtypeStruct((B, MLA_HEADS, S, LANES), BF16),
                   jax.ShapeDtypeStruct((B, MLA_HEADS, S // tk, V_ROWS, tk), BF16),
                   jax.ShapeDtypeStruct((B, NA_HEADS, S // NA_QTOK, LANES, NA_QTOK), BF16),
                   jax.ShapeDtypeStruct((B, hp, S, LANES), BF16),
                   jax.ShapeDtypeStruct((B, NA_HEADS, S // NA_QTOK, V_ROWS, NA_QTOK), BF16)],
        compiler_params=pltpu.CompilerParams(
            dimension_semantics=("parallel", "parallel"), vmem_limit_bytes=VMEM_LIMIT),
        name="proj",
    )(x, *consts, cos_tab, sin_tab, cos_t, sin_t)


MLA_STRIP = 32


def _mla_kernel(qt_ref, k_ref, vt_ref, o_ref, s_ref, p_ref, acc_ref, st_ref):
    nq, tq = qt_ref.shape[2], qt_ref.shape[4]
    nk, tk = vt_ref.shape[2], vt_ref.shape[4]
    assert nk % 2 == 0 and tk % MLA_STRIP == 0
    i = pl.program_id(2)
    qi = jnp.minimum(i, nq - 1)

    def scores(qj, c, hd, slot):
        ks = pl.ds(pl.multiple_of(c * tk, tk), tk)
        st = _dot(k_ref[0, hd, ks, :], qt_ref[0, hd, qj])
        s_ref[slot, hd] = st
        mx = st[0:MLA_STRIP]
        for r in range(MLA_STRIP, tk, MLA_STRIP):
            mx = jnp.maximum(mx, st[r:r + MLA_STRIP])
        return jnp.max(mx, axis=0, keepdims=True)

    def numerators(hd, slot, m, mx):
        m_new = jnp.maximum(m, mx)
        alpha = jnp.exp2(m - m_new)
        for r in range(0, tk, MLA_STRIP):
            st = s_ref[slot, hd, r:r + MLA_STRIP, :]
            p_ref[slot, hd, r:r + MLA_STRIP, :] = jnp.exp2(st - m_new).astype(BF16)
        return m_new, alpha

    def accumulate(c, hd, slot, alpha):
        acc_ref[hd] = alpha * acc_ref[hd] + _dot(vt_ref[0, hd, c], p_ref[slot, hd])

    def chunk(c, cur, carry):
        nxt = 1 - cur
        wraps = c + 1 == nk
        qj_next = jnp.where(wraps, jnp.minimum(i + 1, nq - 1), qi)
        c_next = jnp.where(wraps, 0, c + 1)
        c_prev = jnp.where(c == 0, nk - 1, c - 1)
        out = []
        for hd in range(2):
            m, alpha_prev, mx = carry[hd]
            mx_next = scores(qj_next, c_next, hd, nxt)
            m_new, alpha = numerators(hd, cur, m, mx)
            accumulate(c_prev, hd, nxt, alpha_prev)
            out.append((m_new, alpha, mx_next))
        return tuple(out)

    def step(c, carry):
        carry = lax.cond(c % 2 == 0, lambda: chunk(c, 0, carry), lambda: chunk(c, 1, carry))

        @pl.when((c == 0) & (i > 0))
        def _():
            outs = []
            for hd in range(2):
                acc = acc_ref[hd]
                outs.append(acc[:V_DIM] / acc[V_DIM:V_DIM + 1])
            o_ref[0] = jnp.concatenate(outs, axis=0).T.astype(o_ref.dtype)

        return carry

    @pl.when(i == 0)
    def _():
        for hd in range(2):
            st_ref[hd, 0] = scores(0, 0, hd, 0)
            st_ref[hd, 1] = jnp.ones((1, tq), F32)
            p_ref[1, hd] = jnp.zeros((tk, tq), BF16)
            acc_ref[hd] = jnp.zeros((V_ROWS, tq), F32)

    init = tuple((jnp.full((1, tq), -jnp.inf, F32), st_ref[hd, 1], st_ref[hd, 0]) for hd in range(2))
    fin = lax.fori_loop(0, jnp.where(i == nq, 1, nk), step, init)
    for hd in range(2):
        st_ref[hd, 0] = fin[hd][2]
        st_ref[hd, 1] = fin[hd][1]


def _mla(qt, k, vt):
    B, H, S, _ = k.shape
    nq, tq = qt.shape[2], qt.shape[4]
    nk, tk = vt.shape[2], vt.shape[4]
    once = dict(pipeline_mode=pl.Buffered(1))
    return pl.pallas_call(
        _mla_kernel,
        grid=(B, H // 2, nq + 1),
        in_specs=[pl.BlockSpec((1, 2, nq, LANES, tq), lambda b, p, i: (b, p, 0, 0, 0), **once),
                  pl.BlockSpec((1, 2, S, LANES), lambda b, p, i: (b, p, 0, 0), **once),
                  pl.BlockSpec((1, 2, nk, V_ROWS, tk), lambda b, p, i: (b, p, 0, 0, 0), **once)],
        out_specs=pl.BlockSpec((1, tq, LANES), lambda b, p, i: (b, jnp.maximum(i - 1, 0), p)),
        out_shape=jax.ShapeDtypeStruct((B, S, MLA_WIDTH), BF16),
        scratch_shapes=[pltpu.VMEM((2, 2, tk, tq), F32), pltpu.VMEM((2, 2, tk, tq), BF16),
                        pltpu.VMEM((2, V_ROWS, tq), F32), pltpu.VMEM((2, 2, 1, tq), F32)],
        compiler_params=pltpu.CompilerParams(
            dimension_semantics=("parallel", "parallel", "arbitrary"), vmem_limit_bytes=VMEM_LIMIT),
        name="mla_flash",
    )(qt, k, vt)


NA_QROWS = 4
NA_BROWS = NA_QROWS + NA_WIN_R
NA_QTOK = NA_QROWS * GRID_W
NA_BTOK = NA_BROWS * GRID_W
NA_STRIP = 32
NA_TILES = NA_BROWS // NA_QROWS


def _na_kernel(qt_ref, k_ref, vt_ref, tab_ref, o_ref, s_ref, p_ref, *, nblocks, nblk):
    i = pl.program_id(2)

    def band(j):
        g = i * nblk + j
        t0 = jnp.clip(g - NA_WIN_R // 2 // NA_QROWS, 0, nblocks - NA_TILES)
        var = jnp.where(g == 0, 0, jnp.where(g == nblocks - 1, 2, 1))
        return t0, var

    def scores(j, slot):
        t0, var = band(j)
        kb = k_ref[0, 0, pl.ds(pl.multiple_of(t0 * NA_QTOK, NA_QTOK), NA_BTOK), :]
        mxs = []
        for hd in range(2):
            st = _dot(kb, qt_ref[0, hd, j]) + tab_ref[var, hd]
            s_ref[slot, hd] = st
            mx = st[0:NA_STRIP]
            for r in range(NA_STRIP, NA_BTOK, NA_STRIP):
                mx = jnp.maximum(mx, st[r:r + NA_STRIP])
            mxs.append(jnp.max(mx, axis=0, keepdims=True))
        return tuple(mxs)

    def numerators(slot, mxs):
        for hd in range(2):
            for r in range(0, NA_BTOK, NA_STRIP):
                st = s_ref[slot, hd, r:r + NA_STRIP, :]
                p_ref[slot, hd, r:r + NA_STRIP, :] = jnp.exp2(st - mxs[hd]).astype(BF16)

    def values(j, slot):
        t0, _ = band(j)
        outs = []
        for hd in range(2):
            acc = _dot(vt_ref[0, hd, t0], p_ref[slot, hd, 0:NA_QTOK, :])
            for t in range(1, NA_TILES):
                acc = acc + _dot(vt_ref[0, hd, t0 + t], p_ref[slot, hd, t * NA_QTOK:(t + 1) * NA_QTOK, :])
            outs.append(acc[:NA_DIM] / acc[NA_DIM:NA_DIM + 1])
        rows = pl.ds(pl.multiple_of(j * NA_QTOK, NA_QTOK), NA_QTOK)
        o_ref[0, rows, :] = jnp.concatenate(outs, axis=0).T.astype(o_ref.dtype)

    def block(j, cur, mxs):
        mxs_next = scores(jnp.minimum(j + 1, nblk - 1), 1 - cur)
        numerators(cur, mxs)
        values(jnp.maximum(j - 1, 0), 1 - cur)
        return mxs_next

    def step(j, mxs):
        return lax.cond(j % 2 == 0, lambda: block(j, 0, mxs), lambda: block(j, 1, mxs))

    mxs0 = scores(0, 0)
    p_ref[1] = jnp.ones(p_ref.shape[1:], BF16)
    lax.fori_loop(0, nblk, step, mxs0)
    values(nblk - 1, (nblk - 1) % 2)


def _natten(qt, k, vt, tab, nblk):
    B, HP, S, _ = k.shape
    nblocks = S // NA_QTOK
    assert nblocks >= NA_TILES and nblocks % nblk == 0 and nblk >= 2
    kern = functools.partial(_na_kernel, nblocks=nblocks, nblk=nblk)
    return pl.pallas_call(
        kern,
        grid=(HP, B, nblocks // nblk),
        in_specs=[pl.BlockSpec((1, 2, nblk, LANES, NA_QTOK), lambda p, b, i: (b, p, i, 0, 0)),
                  pl.BlockSpec((1, 1, S, LANES), lambda p, b, i: (b, p, 0, 0)),
                  pl.BlockSpec((1, 2, nblocks, V_ROWS, NA_QTOK), lambda p, b, i: (b, p, 0, 0, 0)),
                  pl.BlockSpec((3, 2, NA_BTOK, NA_QTOK), lambda p, b, i: (0, p, 0, 0))],
        out_specs=pl.BlockSpec((1, nblk * NA_QTOK, LANES), lambda p, b, i: (b, i, p)),
        out_shape=jax.ShapeDtypeStruct((B, S, NA_WIDTH), BF16),
        scratch_shapes=[pltpu.VMEM((2, 2, NA_BTOK, NA_QTOK), F32), pltpu.VMEM((2, 2, NA_BTOK, NA_QTOK), BF16)],
        compiler_params=pltpu.CompilerParams(
            dimension_semantics=("parallel", "parallel", "arbitrary"), vmem_limit_bytes=VMEM_LIMIT),
        name="natten",
    )(qt, k, vt, tab)


def _oproj_kernel(a_ref, n_ref, x_ref, wa_ref, wn_ref, g_ref, o_ref):
    mix = _dot(a_ref[0], wa_ref[...]) + _dot(n_ref[0], wn_ref[...])
    o_ref[0] = x_ref[0] + _rms(mix, g_ref[...])


def _oproj(a, n, x, wts, tm):
    B, S, _ = x.shape
    tok = lambda w: pl.BlockSpec((1, tm, w), lambda b, i: (b, i, 0))
    consts = [wts["w_o_a"], wts["w_o_n"], wts["g_mix_post"]]
    return pl.pallas_call(
        _oproj_kernel,
        grid=(B, S // tm),
        in_specs=[tok(MLA_WIDTH), tok(NA_WIDTH), tok(D_MODEL)] + [_const_spec(c.shape) for c in consts],
        out_specs=tok(D_MODEL),
        out_shape=jax.ShapeDtypeStruct((B, S, D_MODEL), F32),
        compiler_params=pltpu.CompilerParams(
            dimension_semantics=("parallel", "parallel"), vmem_limit_bytes=VMEM_LIMIT),
        name="out_proj",
    )(a, n, x, *consts)


def _gelu_tanh(x):
    return 0.5 * x * (1.0 + jnp.tanh(0.7978845608028654 * (x + 0.044715 * (x * x * x))))


def _ffn_kernel(x_ref, prev_ref, next_ref, g_pre_ref, w_up_ref, cw_ref, cb_ref, w_dn_ref, g_post_ref, o_ref, *, fc):
    i = pl.program_id(1)
    last = pl.num_programs(1) - 1
    x = x_ref[0]
    tm = x.shape[0]
    prev = jnp.where(i > 0, prev_ref[0], 0.0)
    nxt = jnp.where(i < last, next_ref[0], 0.0)
    xh = jnp.concatenate([prev, x, nxt], axis=0)
    hn = _rms(xh, g_pre_ref[...]).astype(BF16)
    n_ext = tm + 2 * HALO

    def conv(hx, col):
        w = cw_ref[:, col:col + fc]
        left = pltpu.roll(hx, 1, 0)[HALO:HALO + tm]
        right = pltpu.roll(hx, n_ext - 1, 0)[HALO:HALO + tm]
        return cb_ref[:, col:col + fc] + left * w[0:1] + hx[HALO:HALO + tm] * w[1:2] + right * w[2:3]

    acc = jnp.zeros((tm, D_MODEL), F32)
    for c in range(D_FF // fc):
        cg = c * fc
        cu = D_FF + c * fc
        g = conv(_dot(hn, w_up_ref[:, cg:cg + fc]), cg)
        u = conv(_dot(hn, w_up_ref[:, cu:cu + fc]), cu)
        act = (_gelu_tanh(g) * u).astype(BF16)
        acc = acc + _dot(act, w_dn_ref[cg:cg + fc, :])
    o_ref[0] = x + _rms(acc, g_post_ref[...])


def _ffn(x, wts, tm, fc):
    B, S, _ = x.shape
    nb = tm // HALO
    kern = functools.partial(_ffn_kernel, fc=fc)
    consts = [wts["g_ffn_pre"], wts["w_ffn_up"], wts["ffn_conv_w"], wts["ffn_conv_b"], wts["w_ffn_down"],
              wts["g_ffn_post"]]
    return pl.pallas_call(
        kern,
        grid=(B, S // tm),
        in_specs=[pl.BlockSpec((1, tm, D_MODEL), lambda b, i: (b, i, 0)),
                  pl.BlockSpec((1, HALO, D_MODEL), lambda b, i: (b, jnp.maximum(i * nb - 1, 0), 0)),
                  pl.BlockSpec((1, HALO, D_MODEL), lambda b, i: (b, jnp.minimum((i + 1) * nb, S // HALO - 1), 0))]
        + [pl.BlockSpec(c.shape, lambda b, i, nd=c.ndim: (0,) * nd, pipeline_mode=pl.Buffered(1)) for c in consts],
        out_specs=pl.BlockSpec((1, tm, D_MODEL), lambda b, i: (b, i, 0)),
        out_shape=jax.ShapeDtypeStruct((B, S, D_MODEL), F32),
        compiler_params=pltpu.CompilerParams(
            dimension_semantics=("parallel", "parallel"), vmem_limit_bytes=VMEM_LIMIT),
        name="conv_ffn",
    )(x, x, x, *consts)


def _prep_weights(g_mix_pre, w_in, g_q_lat, w_q_up, g_kv_lat, w_kv_up, w_o, g_mix_post,
                  g_ffn_pre, w_ffn_up, ffn_conv_w, ffn_conv_b, w_ffn_down, g_ffn_post):
    half = QK_ROPE // 2
    o1, o2, o3 = Q_LORA, Q_LORA + KV_LORA, Q_LORA + KV_LORA + QK_ROPE
    row = lambda g: g.reshape(1, -1).astype(F32)

    w_kr = w_in[:, o2:o3]
    zl = jnp.zeros((D_MODEL, QK_NOPE), F32)
    zr = jnp.zeros((D_MODEL, LANES - QK_NOPE - QK_ROPE), F32)
    w_kr_blk = jnp.concatenate([zl, w_kr, zr], axis=1)
    w_krrot_blk = jnp.concatenate([zl, -w_kr[:, half:], w_kr[:, :half], zr], axis=1)
    w_lat = jnp.concatenate([w_in[:, :o2], w_kr_blk, w_krrot_blk], axis=1)

    wq = w_q_up.reshape(Q_LORA, MLA_HEADS, QK_NOPE + QK_ROPE)
    zq = jnp.zeros((Q_LORA, MLA_HEADS, LANES - QK_NOPE - QK_ROPE), F32)
    w_q = jnp.concatenate([wq, zq], axis=2).reshape(Q_LORA, MLA_HEADS * LANES)
    w_qrot = jnp.concatenate([jnp.zeros((Q_LORA, MLA_HEADS, QK_NOPE), F32), -wq[:, :, QK_NOPE + half:],
                              wq[:, :, QK_NOPE:QK_NOPE + half], zq], axis=2).reshape(Q_LORA, MLA_HEADS * LANES)

    wkv = w_kv_up.reshape(KV_LORA, MLA_HEADS, QK_NOPE + V_DIM)
    w_k = jnp.concatenate([wkv[:, :, :QK_NOPE], jnp.zeros((KV_LORA, MLA_HEADS, LANES - QK_NOPE), F32)],
                          axis=2).reshape(KV_LORA, MLA_HEADS * LANES)
    w_v = jnp.concatenate([wkv[:, :, QK_NOPE:], jnp.zeros((KV_LORA, MLA_HEADS, V_ROWS - V_DIM), F32)],
                          axis=2).reshape(KV_LORA, MLA_HEADS * V_ROWS)

    w_nv = jnp.concatenate([w_in[:, o3 + 2 * NA_WIDTH:].reshape(D_MODEL, NA_HEADS, NA_DIM),
                            jnp.zeros((D_MODEL, NA_HEADS, V_ROWS - NA_DIM), F32)],
                           axis=2).reshape(D_MODEL, NA_HEADS * V_ROWS)

    return {
        "g_mix_pre": row(g_mix_pre), "w_lat": w_lat.astype(BF16),
        "w_nqt": w_in[:, o3:o3 + NA_WIDTH].T.astype(BF16),
        "w_nk": w_in[:, o3 + NA_WIDTH:o3 + 2 * NA_WIDTH].astype(BF16),
        "w_nvt": w_nv.T.astype(BF16),
        "g_q_lat": row(g_q_lat), "w_qt": w_q.T.astype(BF16), "w_qrott": w_qrot.T.astype(BF16),
        "g_kv_lat": row(g_kv_lat), "w_k": w_k.astype(BF16), "w_vt": w_v.T.astype(BF16),
        "w_o_a": w_o[:MLA_WIDTH].astype(BF16), "w_o_n": w_o[MLA_WIDTH:].astype(BF16), "g_mix_post": row(g_mix_post),
        "g_ffn_pre": row(g_ffn_pre), "w_ffn_up": w_ffn_up.astype(BF16), "ffn_conv_w": ffn_conv_w.astype(F32),
        "ffn_conv_b": row(ffn_conv_b), "w_ffn_down": w_ffn_down.astype(BF16), "g_ffn_post": row(g_ffn_post),
    }


def _rope_tables(S):
    inv = 1.0 / (ROPE_THETA ** (jnp.arange(0, QK_ROPE, 2, dtype=F32) / QK_ROPE))
    ang = jnp.arange(S, dtype=F32)[:, None] * inv[None, :]
    cos, sin = jnp.cos(ang), jnp.sin(ang)
    pad = jnp.zeros((S, LANES - QK_NOPE - QK_ROPE), F32)
    cos_tab = jnp.concatenate([jnp.ones((S, QK_NOPE), F32), cos, cos, pad], axis=1)
    sin_tab = jnp.concatenate([jnp.zeros((S, QK_NOPE), F32), sin, sin, pad], axis=1)
    return cos_tab, sin_tab, cos_tab.T, sin_tab.T


def _na_tables(rpb):
    qc = jnp.arange(GRID_W)[:, None]
    kc = jnp.arange(GRID_W)[None, :]
    dc = jnp.clip(kc - qc + (NA_WIN_C - 1), 0, 2 * NA_WIN_C - 2)
    qs = jnp.clip(qc - NA_WIN_C // 2, 0, GRID_W - NA_WIN_C)
    col_valid = (kc >= qs) & (kc < qs + NA_WIN_C)
    rpb = rpb.astype(F32)
    tile = jnp.zeros(rpb.shape[:2] + (GRID_W, GRID_W), F32)
    for e in range(2 * NA_WIN_C - 1):
        tile = jnp.where(dc == e, rpb[:, :, e][:, :, None, None], tile)
    tile = jnp.where(col_valid, tile * LOG2_E, NEG_INF)
    outside = jnp.full((NA_HEADS, GRID_W, GRID_W), NEG_INF, F32)
    variants = []
    for shift, lo in ((0, lambda j: 0), (-NA_WIN_R // 2, lambda j: j), (-NA_WIN_R, lambda j: NA_BROWS - NA_WIN_R)):
        qrows = []
        for j in range(NA_QROWS):
            blocks = [tile[:, i - j + shift + NA_WIN_R - 1] if lo(j) <= i < lo(j) + NA_WIN_R else outside
                      for i in range(NA_BROWS)]
            qrows.append(jnp.concatenate(blocks, axis=2))
        variants.append(jnp.concatenate(qrows, axis=1))
    return jnp.swapaxes(jnp.stack(variants), 2, 3)


def _layer(x, wts, na_tab):
    B, S, _ = x.shape
    tm = 512
    qt, k, vt, nqt, nk, nvt = _proj(x, wts, _rope_tables(S), tm, tk=min(2048, S // 2))
    a = _mla(qt, k, vt)
    n = _natten(nqt, nk, nvt, na_tab, nblk=min(16, S // NA_QTOK))
    x1 = _oproj(a, n, x, wts, tm)
    return _ffn(x1, wts, tm, fc=256)


def kernel(x_prompt, x_sample, g_mix_pre, w_in, g_q_lat, w_q_up, g_kv_lat, w_kv_up, na_rpb, w_o, g_mix_post,
           g_ffn_pre, w_ffn_up, ffn_conv_w, ffn_conv_b, w_ffn_down, g_ffn_post):
    y_prompt, y_sample = x_prompt, x_sample
    for l in range(g_mix_pre.shape[0]):
        wts = _prep_weights(g_mix_pre[l], w_in[l], g_q_lat[l], w_q_up[l], g_kv_lat[l], w_kv_up[l], w_o[l],
                            g_mix_post[l], g_ffn_pre[l], w_ffn_up[l], ffn_conv_w[l], ffn_conv_b[l],
                            w_ffn_down[l], g_ffn_post[l])
        na_tab = _na_tables(na_rpb[l])
        y_prompt = _layer(y_prompt, wts, na_tab)
        y_sample = _layer(y_sample, wts, na_tab)
    return (y_prompt, y_sample)
```

```python
import functools

import jax
import jax.numpy as jnp
from jax import lax
from jax.experimental import pallas as pl
from jax.experimental.pallas import tpu as pltpu

D_MODEL = 1024
GRID_W = 64
MLA_HEADS = 8
Q_LORA = 256
KV_LORA = 128
QK_NOPE = 64
QK_ROPE = 32
V_DIM = 64
ROPE_THETA = 10000.0
NA_HEADS = 8
NA_DIM = 64
NA_WIN_R = 8
NA_WIN_C = 16
NA_WIDTH = NA_HEADS * NA_DIM
MLA_WIDTH = MLA_HEADS * V_DIM
D_FF = 2816
EPS = 1e-6
NEG_INF = -1e30

LANES = 128
V_ROWS = 80
LOG2_E = 1.4426950408889634
Q_SCALE = (QK_NOPE + QK_ROPE) ** -0.5 * LOG2_E
NA_Q_SCALE = NA_DIM ** -0.5 * LOG2_E
HALO = 8
VMEM_LIMIT = 56 * 1024 * 1024

BF16 = jnp.bfloat16
F32 = jnp.float32


def _rms(x, g):
    return x * lax.rsqrt(jnp.mean(x * x, axis=-1, keepdims=True) + EPS) * g


def _dot(a, b):
    return jnp.dot(a, b, preferred_element_type=F32)


def _const_spec(shape):
    return pl.BlockSpec(shape, lambda *_: (0,) * len(shape))


def _proj_kernel(x_ref, g_pre_ref, w_lat_ref, w_nqt_ref, w_nk_ref, w_nvt_ref, g_q_ref, w_qt_ref, w_qrott_ref,
                 g_kv_ref, w_k_ref, w_vt_ref, cos_ref, sin_ref, cost_ref, sint_ref,
                 qt_ref, k_ref, vt_ref, nqt_ref, nk_ref, nvt_ref):
    hn = _rms(x_ref[0], g_pre_ref[...])
    h = hn.astype(BF16)
    tm = h.shape[0]
    cos = cos_ref[...]
    sin = sin_ref[...]
    cost = cost_ref[...]
    sint = sint_ref[...]

    z = _dot(h, w_lat_ref[...])
    c_q = z[:, :Q_LORA]
    c_kv = z[:, Q_LORA:Q_LORA + KV_LORA]
    k_r = z[:, Q_LORA + KV_LORA:Q_LORA + KV_LORA + LANES]
    k_rrot = z[:, Q_LORA + KV_LORA + LANES:]
    k_pe = k_r * cos + k_rrot * sin

    cqn = _rms(c_q, g_q_ref[...])
    cqnt = cqn.T.astype(BF16)
    qmt = _dot(w_qt_ref[...], cqnt)
    qrt = _dot(w_qrott_ref[...], cqnt)
    ckvn = _rms(c_kv, g_kv_ref[...])
    kn = _dot(ckvn.astype(BF16), w_k_ref[...])
    vvt = _dot(w_vt_ref[...], ckvn.T.astype(BF16))
    ones_row = lax.broadcasted_iota(jnp.int32, (V_ROWS, tm), 0) == V_DIM
    for hd in range(MLA_HEADS):
        sl = slice(hd * LANES, (hd + 1) * LANES)
        qt_ref[0, hd, 0] = ((qmt[sl] * cost + qrt[sl] * sint) * Q_SCALE).astype(BF16)
        k_ref[0, hd] = (kn[:, sl] + k_pe).astype(BF16)
        vt_ref[0, hd, 0] = jnp.where(ones_row, 1.0, vvt[hd * V_ROWS:(hd + 1) * V_ROWS]).astype(BF16)

    ht = hn.T.astype(BF16)
    nk = _dot(h, w_nk_ref[...])
    nqt = _dot(w_nqt_ref[...], ht) * NA_Q_SCALE
    nvt = _dot(w_nvt_ref[...], ht)
    low_rows = lax.broadcasted_iota(jnp.int32, (LANES, NA_QTOK), 0) < NA_DIM
    na_ones_row = lax.broadcasted_iota(jnp.int32, (V_ROWS, NA_QTOK), 0) == NA_DIM
    for p in range(NA_HEADS // 2):
        nk_ref[0, p] = nk[:, p * LANES:(p + 1) * LANES].astype(BF16)
        for t in range(tm // NA_QTOK):
            cols = slice(t * NA_QTOK, (t + 1) * NA_QTOK)
            pair = nqt[p * LANES:(p + 1) * LANES, cols]
            nqt_ref[0, 2 * p, t] = jnp.where(low_rows, pair, 0.0).astype(BF16)
            nqt_ref[0, 2 * p + 1, t] = jnp.where(low_rows, 0.0, pair).astype(BF16)
            for hd in (2 * p, 2 * p + 1):
                nvt_ref[0, hd, t] = jnp.where(na_ones_row, 1.0, nvt[hd * V_ROWS:(hd + 1) * V_ROWS, cols]).astype(BF16)


def _proj(x, wts, rope, tm, tk):
    B, S, _ = x.shape
    nt = S // tm
    per = tk // tm
    nqt = tm // NA_QTOK
    hp = NA_HEADS // 2
    head_out = lambda n: pl.BlockSpec((1, n, tm, LANES), lambda b, i: (b, 0, i, 0))
    names = ("g_mix_pre", "w_lat", "w_nqt", "w_nk", "w_nvt", "g_q_lat", "w_qt", "w_qrott", "g_kv_lat", "w_k", "w_vt")
    consts = [wts[n] for n in names]
    cos_tab, sin_tab, cos_t, sin_t = rope
    return pl.pallas_call(
        _proj_kernel,
        grid=(B, nt),
        in_specs=[pl.BlockSpec((1, tm, D_MODEL), lambda b, i: (b, i, 0))]
        + [_const_spec(c.shape) for c in consts]
        + [pl.BlockSpec((tm, LANES), lambda b, i: (i, 0))] * 2
        + [pl.BlockSpec((LANES, tm), lambda b, i: (0, i))] * 2,
        out_specs=[pl.BlockSpec((1, MLA_HEADS, 1, LANES, tm), lambda b, i: (b, 0, i, 0, 0)),
                   head_out(MLA_HEADS),
                   pl.BlockSpec((1, MLA_HEADS, 1, V_ROWS, tm), lambda b, i: (b, 0, i // per, 0, i % per)),
                   pl.BlockSpec((1, NA_HEADS, nqt, LANES, NA_QTOK), lambda b, i: (b, 0, i, 0, 0)),
                   head_out(hp),
                   pl.BlockSpec((1, NA_HEADS, nqt, V_ROWS, NA_QTOK), lambda b, i: (b, 0, i, 0, 0))],
        out_shape=[jax.ShapeDtypeStruct((B, MLA_HEADS, nt, LANES, tm), BF16),
                   jax.ShapeDtypeStruct((B, MLA_HEADS, S, LANES), BF16),
                   jax.ShapeDtypeStruct((B, MLA_HEADS, S // tk, V_ROWS, tk), BF16),
                   jax.ShapeDtypeStruct((B, NA_HEADS, S // NA_QTOK, LANES, NA_QTOK), BF16),
                   jax.ShapeDtypeStruct((B, hp, S, LANES), BF16),
                   jax.ShapeDtypeStruct((B, NA_HEADS, S // NA_QTOK, V_ROWS, NA_QTOK), BF16)],
        compiler_params=pltpu.CompilerParams(
            dimension_semantics=("parallel", "parallel"), vmem_limit_bytes=VMEM_LIMIT),
        name="proj",
    )(x, *consts, cos_tab, sin_tab, cos_t, sin_t)


MLA_STRIP = 32


def _mla_kernel(qt_ref, k_ref, vt_ref, o_ref, s_ref, p_ref, acc_ref, st_ref):
    nq, tq = qt_ref.shape[2], qt_ref.shape[4]
    nk, tk = vt_ref.shape[2], vt_ref.shape[4]
    assert nk % 2 == 0 and tk % MLA_STRIP == 0
    i = pl.program_id(2)
    qi = jnp.minimum(i, nq - 1)

    def scores(qj, c, hd, slot):
        ks = pl.ds(pl.multiple_of(c * tk, tk), tk)
        st = _dot(k_ref[0, hd, ks, :], qt_ref[0, hd, qj])
        s_ref[slot, hd] = st
        mx = st[0:MLA_STRIP]
        for r in range(MLA_STRIP, tk, MLA_STRIP):
            mx = jnp.maximum(mx, st[r:r + MLA_STRIP])
        return jnp.max(mx, axis=0, keepdims=True)

    def numerators(hd, slot, m, mx):
        m_new = jnp.maximum(m, mx)
        alpha = jnp.exp2(m - m_new)
        for r in range(0, tk, MLA_STRIP):
            st = s_ref[slot, hd, r:r + MLA_STRIP, :]
            p_ref[slot, hd, r:r + MLA_STRIP, :] = jnp.exp2(st - m_new).astype(BF16)
        return m_new, alpha

    def accumulate(c, hd, slot, alpha):
        acc_ref[hd] = alpha * acc_ref[hd] + _dot(vt_ref[0, hd, c], p_ref[slot, hd])

    def chunk(c, cur, carry):
        nxt = 1 - cur
        wraps = c + 1 == nk
        qj_next = jnp.where(wraps, jnp.minimum(i + 1, nq - 1), qi)
        c_next = jnp.where(wraps, 0, c + 1)
        c_prev = jnp.where(c == 0, nk - 1, c - 1)
        out = []
        for hd in range(2):
            m, alpha_prev, mx = carry[hd]
            mx_next = scores(qj_next, c_next, hd, nxt)
            m_new, alpha = numerators(hd, cur, m, mx)
            accumulate(c_prev, hd, nxt, alpha_prev)
            out.append((m_new, alpha, mx_next))
        return tuple(out)

    def step(c, carry):
        carry = lax.cond(c % 2 == 0, lambda: chunk(c, 0, carry), lambda: chunk(c, 1, carry))

        @pl.when((c == 0) & (i > 0))
        def _():
            outs = []
            for hd in range(2):
                acc = acc_ref[hd]
                outs.append(acc[:V_DIM] / acc[V_DIM:V_DIM + 1])
            o_ref[0] = jnp.concatenate(outs, axis=0).T.astype(o_ref.dtype)

        return carry

    @pl.when(i == 0)
    def _():
        for hd in range(2):
            st_ref[hd, 0] = scores(0, 0, hd, 0)
            st_ref[hd, 1] = jnp.ones((1, tq), F32)
            p_ref[1, hd] = jnp.zeros((tk, tq), BF16)
            acc_ref[hd] = jnp.zeros((V_ROWS, tq), F32)

    init = tuple((jnp.full((1, tq), -jnp.inf, F32), st_ref[hd, 1], st_ref[hd, 0]) for hd in range(2))
    fin = lax.fori_loop(0, jnp.where(i == nq, 1, nk), step, init)
    for hd in range(2):
        st_ref[hd, 0] = fin[hd][2]
        st_ref[hd, 1] = fin[hd][1]


def _mla(qt, k, vt):
    B, H, S, _ = k.shape
    nq, tq = qt.shape[2], qt.shape[4]
    nk, tk = vt.shape[2], vt.shape[4]
    once = dict(pipeline_mode=pl.Buffered(1))
    return pl.pallas_call(
        _mla_kernel,
        grid=(B, H // 2, nq + 1),
        in_specs=[pl.BlockSpec((1, 2, nq, LANES, tq), lambda b, p, i: (b, p, 0, 0, 0), **once),
                  pl.BlockSpec((1, 2, S, LANES), lambda b, p, i: (b, p, 0, 0), **once),
                  pl.BlockSpec((1, 2, nk, V_ROWS, tk), lambda b, p, i: (b, p, 0, 0, 0), **once)],
        out_specs=pl.BlockSpec((1, tq, LANES), lambda b, p, i: (b, jnp.maximum(i - 1, 0), p)),
        out_shape=jax.ShapeDtypeStruct((B, S, MLA_WIDTH), BF16),
        scratch_shapes=[pltpu.VMEM((2, 2, tk, tq), F32), pltpu.VMEM((2, 2, tk, tq), BF16),
                        pltpu.VMEM((2, V_ROWS, tq), F32), pltpu.VMEM((2, 2, 1, tq), F32)],
        compiler_params=pltpu.CompilerParams(
            dimension_semantics=("parallel", "parallel", "arbitrary"), vmem_limit_bytes=VMEM_LIMIT),
        name="mla_flash",
    )(qt, k, vt)


NA_QROWS = 4
NA_BROWS = NA_QROWS + NA_WIN_R
NA_QTOK = NA_QROWS * GRID_W
NA_BTOK = NA_BROWS * GRID_W
NA_STRIP = 32
NA_TILES = NA_BROWS // NA_QROWS


def _na_kernel(qt_ref, k_ref, vt_ref, tab_ref, o_ref, s_ref, p_ref, *, nblocks, nblk):
    i = pl.program_id(2)

    def band(j):
        g = i * nblk + j
        t0 = jnp.clip(g - NA_WIN_R // 2 // NA_QROWS, 0, nblocks - NA_TILES)
        var = jnp.where(g == 0, 0, jnp.where(g == nblocks - 1, 2, 1))
        return t0, var

    def scores(j, slot):
        t0, var = band(j)
        kb = k_ref[0, 0, pl.ds(pl.multiple_of(t0 * NA_QTOK, NA_QTOK), NA_BTOK), :]
        mxs = []
        for hd in range(2):
            st = _dot(kb, qt_ref[0, hd, j]) + tab_ref[var, hd]
            s_ref[slot, hd] = st
            mx = st[0:NA_STRIP]
            for r in range(NA_STRIP, NA_BTOK, NA_STRIP):
                mx = jnp.maximum(mx, st[r:r + NA_STRIP])
            mxs.append(jnp.max(mx, axis=0, keepdims=True))
        return tuple(mxs)

    def numerators(slot, mxs):
        for hd in range(2):
            for r in range(0, NA_BTOK, NA_STRIP):
                st = s_ref[slot, hd, r:r + NA_STRIP, :]
                p_ref[slot, hd, r:r + NA_STRIP, :] = jnp.exp2(st - mxs[hd]).astype(BF16)

    def values(j, slot):
        t0, _ = band(j)
        outs = []
        for hd in range(2):
            acc = _dot(vt_ref[0, hd, t0], p_ref[slot, hd, 0:NA_QTOK, :])
            for t in range(1, NA_TILES):
                acc = acc + _dot(vt_ref[0, hd, t0 + t], p_ref[slot, hd, t * NA_QTOK:(t + 1) * NA_QTOK, :])
            outs.append(acc[:NA_DIM] / acc[NA_DIM:NA_DIM + 1])
        rows = pl.ds(pl.multiple_of(j * NA_QTOK, NA_QTOK), NA_QTOK)
        o_ref[0, rows, :] = jnp.concatenate(outs, axis=0).T.astype(o_ref.dtype)

    def block(j, cur, mxs):
        mxs_next = scores(jnp.minimum(j + 1, nblk - 1), 1 - cur)
        numerators(cur, mxs)
        values(jnp.maximum(j - 1, 0), 1 - cur)
        return mxs_next

    def step(j, mxs):
        return lax.cond(j % 2 == 0, lambda: block(j, 0, mxs), lambda: block(j, 1, mxs))

    mxs0 = scores(0, 0)
    p_ref[1] = jnp.ones(p_ref.shape[1:], BF16)
    lax.fori_loop(0, nblk, step, mxs0)
    values(nblk - 1, (nblk - 1) % 2)


def _natten(qt, k, vt, tab, nblk):
    B, HP, S, _ = k.shape
    nblocks = S // NA_QTOK
    assert nblocks >= NA_TILES and nblocks % nblk == 0 and nblk >= 2
    kern = functools.partial(_na_kernel, nblocks=nblocks, nblk=nblk)
    return pl.pallas_call(
        kern,
        grid=(HP, B, nblocks // nblk),
        in_specs=[pl.BlockSpec((1, 2, nblk, LANES, NA_QTOK), lambda p, b, i: (b, p, i, 0, 0)),
                  pl.BlockSpec((1, 1, S, LANES), lambda p, b, i: (b, p, 0, 0)),
                  pl.BlockSpec((1, 2, nblocks, V_ROWS, NA_QTOK), lambda p, b, i: (b, p, 0, 0, 0)),
                  pl.BlockSpec((3, 2, NA_BTOK, NA_QTOK), lambda p, b, i: (0, p, 0, 0))],
        out_specs=pl.BlockSpec((1, nblk * NA_QTOK, LANES), lambda p, b, i: (b, i, p)),
        out_shape=jax.ShapeDtypeStruct((B, S, NA_WIDTH), BF16),
        scratch_shapes=[pltpu.VMEM((2, 2, NA_BTOK, NA_QTOK), F32), pltpu.VMEM((2, 2, NA_BTOK, NA_QTOK), BF16)],
        compiler_params=pltpu.CompilerParams(
            dimension_semantics=("parallel", "parallel", "arbitrary"), vmem_limit_bytes=VMEM_LIMIT),
        name="natten",
    )(qt, k, vt, tab)


def _oproj_kernel(a_ref, n_ref, x_ref, wa_ref, wn_ref, g_ref, o_ref):
    mix = _dot(a_ref[0], wa_ref[...]) + _dot(n_ref[0], wn_ref[...])
    o_ref[0] = x_ref[0] + _rms(mix, g_ref[...])


def _oproj(a, n, x, wts, tm):
    B, S, _ = x.shape
    tok = lambda w: pl.BlockSpec((1, tm, w), lambda b, i: (b, i, 0))
    consts = [wts["w_o_a"], wts["w_o_n"], wts["g_mix_post"]]
    return pl.pallas_call(
        _oproj_kernel,
        grid=(B, S // tm),
        in_specs=[tok(MLA_WIDTH), tok(NA_WIDTH), tok(D_MODEL)] + [_const_spec(c.shape) for c in consts],
        out_specs=tok(D_MODEL),
        out_shape=jax.ShapeDtypeStruct((B, S, D_MODEL), F32),
        compiler_params=pltpu.CompilerParams(
            dimension_semantics=("parallel", "parallel"), vmem_limit_bytes=VMEM_LIMIT),
        name="out_proj",
    )(a, n, x, *consts)


def _gelu_tanh(x):
    return 0.5 * x * (1.0 + jnp.tanh(0.7978845608028654 * (x + 0.044715 * (x * x * x))))


def _ffn_kernel(x_ref, prev_ref, next_ref, g_pre_ref, w_up_ref, cw_ref, cb_ref, w_dn_ref, g_post_ref, o_ref, act_ref,
                *, fc):
    i = pl.program_id(1)
    last = pl.num_programs(1) - 1
    x = x_ref[0]
    tm = x.shape[0]
    prev = jnp.where(i > 0, prev_ref[0], 0.0)
    nxt = jnp.where(i < last, next_ref[0], 0.0)
    xh = jnp.concatenate([prev, x, nxt], axis=0)
    hn = _rms(xh, g_pre_ref[...]).astype(BF16)
    n_ext = tm + 2 * HALO

    def conv(hx, col):
        w = cw_ref[:, col:col + fc]
        left = pltpu.roll(hx, 1, 0)[HALO:HALO + tm]
        right = pltpu.roll(hx, n_ext - 1, 0)[HALO:HALO + tm]
        return cb_ref[:, col:col + fc] + left * w[0:1] + hx[HALO:HALO + tm] * w[1:2] + right * w[2:3]

    for c in range(D_FF // fc):
        cg = c * fc
        cu = D_FF + c * fc
        g = conv(_dot(hn, w_up_ref[:, cg:cg + fc]), cg)
        u = conv(_dot(hn, w_up_ref[:, cu:cu + fc]), cu)
        act_ref[:, cg:cg + fc] = (_gelu_tanh(g) * u).astype(BF16)
    o_ref[0] = x + _rms(_dot(act_ref[...], w_dn_ref[...]), g_post_ref[...])


def _ffn(x, wts, tm, fc):
    B, S, _ = x.shape
    nb = tm // HALO
    kern = functools.partial(_ffn_kernel, fc=fc)
    consts = [wts["g_ffn_pre"], wts["w_ffn_up"], wts["ffn_conv_w"], wts["ffn_conv_b"], wts["w_ffn_down"],
              wts["g_ffn_post"]]
    return pl.pallas_call(
        kern,
        grid=(B, S // tm),
        in_specs=[pl.BlockSpec((1, tm, D_MODEL), lambda b, i: (b, i, 0)),
                  pl.BlockSpec((1, HALO, D_MODEL), lambda b, i: (b, jnp.maximum(i * nb - 1, 0), 0)),
                  pl.BlockSpec((1, HALO, D_MODEL), lambda b, i: (b, jnp.minimum((i + 1) * nb, S // HALO - 1), 0))]
        + [pl.BlockSpec(c.shape, lambda b, i, nd=c.ndim: (0,) * nd, pipeline_mode=pl.Buffered(1)) for c in consts],
        out_specs=pl.BlockSpec((1, tm, D_MODEL), lambda b, i: (b, i, 0)),
        out_shape=jax.ShapeDtypeStruct((B, S, D_MODEL), F32),
        scratch_shapes=[pltpu.VMEM((tm, D_FF), BF16)],
        compiler_params=pltpu.CompilerParams(
            dimension_semantics=("parallel", "parallel"), vmem_limit_bytes=VMEM_LIMIT),
        name="conv_ffn",
    )(x, x, x, *consts)


def _prep_weights(g_mix_pre, w_in, g_q_lat, w_q_up, g_kv_lat, w_kv_up, w_o, g_mix_post,
                  g_ffn_pre, w_ffn_up, ffn_conv_w, ffn_conv_b, w_ffn_down, g_ffn_post):
    half = QK_ROPE // 2
    o1, o2, o3 = Q_LORA, Q_LORA + KV_LORA, Q_LORA + KV_LORA + QK_ROPE
    row = lambda g: g.reshape(1, -1).astype(F32)

    w_kr = w_in[:, o2:o3]
    zl = jnp.zeros((D_MODEL, QK_NOPE), F32)
    zr = jnp.zeros((D_MODEL, LANES - QK_NOPE - QK_ROPE), F32)
    w_kr_blk = jnp.concatenate([zl, w_kr, zr], axis=1)
    w_krrot_blk = jnp.concatenate([zl, -w_kr[:, half:], w_kr[:, :half], zr], axis=1)
    w_lat = jnp.concatenate([w_in[:, :o2], w_kr_blk, w_krrot_blk], axis=1)

    wq = w_q_up.reshape(Q_LORA, MLA_HEADS, QK_NOPE + QK_ROPE)
    zq = jnp.zeros((Q_LORA, MLA_HEADS, LANES - QK_NOPE - QK_ROPE), F32)
    w_q = jnp.concatenate([wq, zq], axis=2).reshape(Q_LORA, MLA_HEADS * LANES)
    w_qrot = jnp.concatenate([jnp.zeros((Q_LORA, MLA_HEADS, QK_NOPE), F32), -wq[:, :, QK_NOPE + half:],
                              wq[:, :, QK_NOPE:QK_NOPE + half], zq], axis=2).reshape(Q_LORA, MLA_HEADS * LANES)

    wkv = w_kv_up.reshape(KV_LORA, MLA_HEADS, QK_NOPE + V_DIM)
    w_k = jnp.concatenate([wkv[:, :, :QK_NOPE], jnp.zeros((KV_LORA, MLA_HEADS, LANES - QK_NOPE), F32)],
                          axis=2).reshape(KV_LORA, MLA_HEADS * LANES)
    w_v = jnp.concatenate([wkv[:, :, QK_NOPE:], jnp.zeros((KV_LORA, MLA_HEADS, V_ROWS - V_DIM), F32)],
                          axis=2).reshape(KV_LORA, MLA_HEADS * V_ROWS)

    w_nv = jnp.concatenate([w_in[:, o3 + 2 * NA_WIDTH:].reshape(D_MODEL, NA_HEADS, NA_DIM),
                            jnp.zeros((D_MODEL, NA_HEADS, V_ROWS - NA_DIM), F32)],
                           axis=2).reshape(D_MODEL, NA_HEADS * V_ROWS)

    return {
        "g_mix_pre": row(g_mix_pre), "w_lat": w_lat.astype(BF16),
        "w_nqt": w_in[:, o3:o3 + NA_WIDTH].T.astype(BF16),
        "w_nk": w_in[:, o3 + NA_WIDTH:o3 + 2 * NA_WIDTH].astype(BF16),
        "w_nvt": w_nv.T.astype(BF16),
        "g_q_lat": row(g_q_lat), "w_qt": w_q.T.astype(BF16), "w_qrott": w_qrot.T.astype(BF16),
        "g_kv_lat": row(g_kv_lat), "w_k": w_k.astype(BF16), "w_vt": w_v.T.astype(BF16),
        "w_o_a": w_o[:MLA_WIDTH].astype(BF16), "w_o_n": w_o[MLA_WIDTH:].astype(BF16), "g_mix_post": row(g_mix_post),
        "g_ffn_pre": row(g_ffn_pre), "w_ffn_up": w_ffn_up.astype(BF16), "ffn_conv_w": ffn_conv_w.astype(F32),
        "ffn_conv_b": row(ffn_conv_b), "w_ffn_down": w_ffn_down.astype(BF16), "g_ffn_post": row(g_ffn_post),
    }


def _rope_tables(S):
    inv = 1.0 / (ROPE_THETA ** (jnp.arange(0, QK_ROPE, 2, dtype=F32) / QK_ROPE))
    ang = jnp.arange(S, dtype=F32)[:, None] * inv[None, :]
    cos, sin = jnp.cos(ang), jnp.sin(ang)
    pad = jnp.zeros((S, LANES - QK_NOPE - QK_ROPE), F32)
    cos_tab = jnp.concatenate([jnp.ones((S, QK_NOPE), F32), cos, cos, pad], axis=1)
    sin_tab = jnp.concatenate([jnp.zeros((S, QK_NOPE), F32), sin, sin, pad], axis=1)
    return cos_tab, sin_tab, cos_tab.T, sin_tab.T


def _na_tables(rpb):
    qc = jnp.arange(GRID_W)[:, None]
    kc = jnp.arange(GRID_W)[None, :]
    dc = jnp.clip(kc - qc + (NA_WIN_C - 1), 0, 2 * NA_WIN_C - 2)
    qs = jnp.clip(qc - NA_WIN_C // 2, 0, GRID_W - NA_WIN_C)
    col_valid = (kc >= qs) & (kc < qs + NA_WIN_C)
    rpb = rpb.astype(F32)
    tile = jnp.zeros(rpb.shape[:2] + (GRID_W, GRID_W), F32)
    for e in range(2 * NA_WIN_C - 1):
        tile = jnp.where(dc == e, rpb[:, :, e][:, :, None, None], tile)
    tile = jnp.where(col_valid, tile * LOG2_E, NEG_INF)
    outside = jnp.full((NA_HEADS, GRID_W, GRID_W), NEG_INF, F32)
    variants = []
    for shift, lo in ((0, lambda j: 0), (-NA_WIN_R // 2, lambda j: j), (-NA_WIN_R, lambda j: NA_BROWS - NA_WIN_R)):
        qrows = []
        for j in range(NA_QROWS):
            blocks = [tile[:, i - j + shift + NA_WIN_R - 1] if lo(j) <= i < lo(j) + NA_WIN_R else outside
                      for i in range(NA_BROWS)]
            qrows.append(jnp.concatenate(blocks, axis=2))
        variants.append(jnp.concatenate(qrows, axis=1))
    return jnp.swapaxes(jnp.stack(variants), 2, 3)


def _layer(x, wts, na_tab):
    B, S, _ = x.shape
    tm = 512
    qt, k, vt, nqt, nk, nvt = _proj(x, wts, _rope_tables(S), tm, tk=min(2048, S // 2))
    a = _mla(qt, k, vt)
    n = _natten(nqt, nk, nvt, na_tab, nblk=min(16, S // NA_QTOK))
    x1 = _oproj(a, n, x, wts, tm)
    return _ffn(x1, wts, tm, fc=256)


def kernel(x_prompt, x_sample, g_mix_pre, w_in, g_q_lat, w_q_up, g_kv_lat, w_kv_up, na_rpb, w_o, g_mix_post,
           g_ffn_pre, w_ffn_up, ffn_conv_w, ffn_conv_b, w_ffn_down, g_ffn_post):
    y_prompt, y_sample = x_prompt, x_sample
    for l in range(g_mix_pre.shape[0]):
        wts = _prep_weights(g_mix_pre[l], w_in[l], g_q_lat[l], w_q_up[l], g_kv_lat[l], w_kv_up[l], w_o[l],
                            g_mix_post[l], g_ffn_pre[l], w_ffn_up[l], ffn_conv_w[l], ffn_conv_b[l],
                            w_ffn_down[l], g_ffn_post[l])
        na_tab = _na_tables(na_rpb[l])
        y_prompt = _layer(y_prompt, wts, na_tab)
        y_sample = _layer(y_sample, wts, na_tab)
    return (y_prompt, y_sample)
```

```python
import functools

import jax
import jax.numpy as jnp
from jax import lax
from jax.experimental import pallas as pl
from jax.experimental.pallas import tpu as pltpu

D_MODEL = 1024
GRID_W = 64
MLA_HEADS = 8
Q_LORA = 256
KV_LORA = 128
QK_NOPE = 64
QK_ROPE = 32
V_DIM = 64
ROPE_THETA = 10000.0
NA_HEADS = 8
NA_DIM = 64
NA_WIN_R = 8
NA_WIN_C = 16
NA_WIDTH = NA_HEADS * NA_DIM
MLA_WIDTH = MLA_HEADS * V_DIM
D_FF = 2816
EPS = 1e-6
NEG_INF = -1e30

LANES = 128
V_ROWS = 80
LOG2_E = 1.4426950408889634
Q_SCALE = (QK_NOPE + QK_ROPE) ** -0.5 * LOG2_E
NA_Q_SCALE = NA_DIM ** -0.5 * LOG2_E
HALO = 8
VMEM_LIMIT = 56 * 1024 * 1024

BF16 = jnp.bfloat16
F32 = jnp.float32


def _rms(x, g):
    return x * lax.rsqrt(jnp.mean(x * x, axis=-1, keepdims=True) + EPS) * g


def _dot(a, b):
    return jnp.dot(a, b, preferred_element_type=F32)


def _const_spec(shape):
    return pl.BlockSpec(shape, lambda *_: (0,) * len(shape))


def _proj_kernel(x_ref, g_pre_ref, w_lat_ref, w_nqt_ref, w_nk_ref, w_nvt_ref, g_q_ref, w_qt_ref,
                 g_kv_ref, w_k_ref, w_vt_ref, cos_ref, sin_ref, cost_ref, sint_ref,
                 qt_ref, k_ref, vt_ref, nqt_ref, nk_ref, nvt_ref):
    hn = _rms(x_ref[0], g_pre_ref[...])
    h = hn.astype(BF16)
    tm = h.shape[0]
    cos = cos_ref[...]
    sin = sin_ref[...]
    cost = cost_ref[...]
    sint = sint_ref[...]

    z = _dot(h, w_lat_ref[...])
    c_q = z[:, :Q_LORA]
    c_kv = z[:, Q_LORA:Q_LORA + KV_LORA]
    k_r = z[:, Q_LORA + KV_LORA:Q_LORA + KV_LORA + LANES]
    k_rrot = z[:, Q_LORA + KV_LORA + LANES:]
    k_pe = k_r * cos + k_rrot * sin

    cqn = _rms(c_q, g_q_ref[...])
    cqnt = cqn.T.astype(BF16)
    qmt = _dot(w_qt_ref[...], cqnt)
    ckvn = _rms(c_kv, g_kv_ref[...])
    kn = _dot(ckvn.astype(BF16), w_k_ref[...])
    vvt = _dot(w_vt_ref[...], ckvn.T.astype(BF16))
    ones_row = lax.broadcasted_iota(jnp.int32, (V_ROWS, tm), 0) == V_DIM
    half = QK_ROPE // 2
    for hd in range(MLA_HEADS):
        sl = slice(hd * LANES, (hd + 1) * LANES)
        qh = qmt[sl]
        qrot = jnp.concatenate([qh[:QK_NOPE], -qh[QK_NOPE + half:QK_NOPE + QK_ROPE], qh[QK_NOPE:QK_NOPE + half],
                                qh[QK_NOPE + QK_ROPE:]], axis=0)
        qt_ref[0, hd, 0] = ((qh * cost + qrot * sint) * Q_SCALE).astype(BF16)
        k_ref[0, hd] = (kn[:, sl] + k_pe).astype(BF16)
        vt_ref[0, hd, 0] = jnp.where(ones_row, 1.0, vvt[hd * V_ROWS:(hd + 1) * V_ROWS]).astype(BF16)

    ht = hn.T.astype(BF16)
    nk = _dot(h, w_nk_ref[...])
    nqt = _dot(w_nqt_ref[...], ht) * NA_Q_SCALE
    nvt = _dot(w_nvt_ref[...], ht)
    low_rows = lax.broadcasted_iota(jnp.int32, (LANES, NA_QTOK), 0) < NA_DIM
    na_ones_row = lax.broadcasted_iota(jnp.int32, (V_ROWS, NA_QTOK), 0) == NA_DIM
    for p in range(NA_HEADS // 2):
        nk_ref[0, p] = nk[:, p * LANES:(p + 1) * LANES].astype(BF16)
        for t in range(tm // NA_QTOK):
            cols = slice(t * NA_QTOK, (t + 1) * NA_QTOK)
            pair = nqt[p * LANES:(p + 1) * LANES, cols]
            nqt_ref[0, 2 * p, t] = jnp.where(low_rows, pair, 0.0).astype(BF16)
            nqt_ref[0, 2 * p + 1, t] = jnp.where(low_rows, 0.0, pair).astype(BF16)
            for hd in (2 * p, 2 * p + 1):
                nvt_ref[0, hd, t] = jnp.where(na_ones_row, 1.0, nvt[hd * V_ROWS:(hd + 1) * V_ROWS, cols]).astype(BF16)


def _proj(x, wts, rope, tm, tk):
    B, S, _ = x.shape
    nt = S // tm
    per = tk // tm
    nqt = tm // NA_QTOK
    hp = NA_HEADS // 2
    head_out = lambda n: pl.BlockSpec((1, n, tm, LANES), lambda b, i: (b, 0, i, 0))
    names = ("g_mix_pre", "w_lat", "w_nqt", "w_nk", "w_nvt", "g_q_lat", "w_qt", "g_kv_lat", "w_k", "w_vt")
    consts = [wts[n] for n in names]
    cos_tab, sin_tab, cos_t, sin_t = rope
    return pl.pallas_call(
        _proj_kernel,
        grid=(B, nt),
        in_specs=[pl.BlockSpec((1, tm, D_MODEL), lambda b, i: (b, i, 0))]
        + [_const_spec(c.shape) for c in consts]
        + [pl.BlockSpec((tm, LANES), lambda b, i: (i, 0))] * 2
        + [pl.BlockSpec((LANES, tm), lambda b, i: (0, i))] * 2,
        out_specs=[pl.BlockSpec((1, MLA_HEADS, 1, LANES, tm), lambda b, i: (b, 0, i, 0, 0)),
                   head_out(MLA_HEADS),
                   pl.BlockSpec((1, MLA_HEADS, 1, V_ROWS, tm), lambda b, i: (b, 0, i // per, 0, i % per)),
                   pl.BlockSpec((1, NA_HEADS, nqt, LANES, NA_QTOK), lambda b, i: (b, 0, i, 0, 0)),
                   head_out(hp),
                   pl.BlockSpec((1, NA_HEADS, nqt, V_ROWS, NA_QTOK), lambda b, i: (b, 0, i, 0, 0))],
        out_shape=[jax.ShapeDtypeStruct((B, MLA_HEADS, nt, LANES, tm), BF16),
                   jax.ShapeDtypeStruct((B, MLA_HEADS, S, LANES), BF16),
                   jax.ShapeDtypeStruct((B, MLA_HEADS, S // tk, V_ROWS, tk), BF16),
                   jax.ShapeDtypeStruct((B, NA_HEADS, S // NA_QTOK, LANES, NA_QTOK), BF16),
                   jax.ShapeDtypeStruct((B, hp, S, LANES), BF16),
                   jax.ShapeDtypeStruct((B, NA_HEADS, S // NA_QTOK, V_ROWS, NA_QTOK), BF16)],
        compiler_params=pltpu.CompilerParams(
            dimension_semantics=("parallel", "parallel"), vmem_limit_bytes=VMEM_LIMIT),
        name="proj",
    )(x, *consts, cos_tab, sin_tab, cos_t, sin_t)


MLA_STRIP = 32


def _mla_kernel(qt_ref, k_ref, vt_ref, o_ref, s_ref, p_ref, acc_ref, st_ref):
    nq, tq = qt_ref.shape[2], qt_ref.shape[4]
    nk, tk = vt_ref.shape[2], vt_ref.shape[4]
    assert nk % 2 == 0 and tk % MLA_STRIP == 0
    i = pl.program_id(2)
    qi = jnp.minimum(i, nq - 1)

    def scores(qj, c, hd, slot):
        ks = pl.ds(pl.multiple_of(c * tk, tk), tk)
        st = _dot(k_ref[0, hd, ks, :], qt_ref[0, hd, qj])
        s_ref[slot, hd] = st
        mx = st[0:MLA_STRIP]
        for r in range(MLA_STRIP, tk, MLA_STRIP):
            mx = jnp.maximum(mx, st[r:r + MLA_STRIP])
        return jnp.max(mx, axis=0, keepdims=True)

    def numerators(hd, slot, m, mx):
        m_new = jnp.maximum(m, mx)
        alpha = jnp.exp2(m - m_new)
        for r in range(0, tk, MLA_STRIP):
            st = s_ref[slot, hd, r:r + MLA_STRIP, :]
            p_ref[slot, hd, r:r + MLA_STRIP, :] = jnp.exp2(st - m_new).astype(BF16)
        return m_new, alpha

    def accumulate(c, hd, slot, alpha):
        acc_ref[hd] = alpha * acc_ref[hd] + _dot(vt_ref[0, hd, c], p_ref[slot, hd])

    def chunk(c, cur, carry):
        nxt = 1 - cur
        wraps = c + 1 == nk
        qj_next = jnp.where(wraps, jnp.minimum(i + 1, nq - 1), qi)
        c_next = jnp.where(wraps, 0, c + 1)
        c_prev = jnp.where(c == 0, nk - 1, c - 1)
        out = []
        for hd in range(2):
            m, alpha_prev, mx = carry[hd]
            mx_next = scores(qj_next, c_next, hd, nxt)
            m_new, alpha = numerators(hd, cur, m, mx)
            accumulate(c_prev, hd, nxt, alpha_prev)
            out.append((m_new, alpha, mx_next))
        return tuple(out)

    def step(c, carry):
        carry = lax.cond(c % 2 == 0, lambda: chunk(c, 0, carry), lambda: chunk(c, 1, carry))

        @pl.when((c == 0) & (i > 0))
        def _():
            outs = []
            for hd in range(2):
                acc = acc_ref[hd]
                outs.append(acc[:V_DIM] / acc[V_DIM:V_DIM + 1])
            o_ref[0] = jnp.concatenate(outs, axis=0).T.astype(o_ref.dtype)

        return carry

    @pl.when(i == 0)
    def _():
        for hd in range(2):
            st_ref[hd, 0] = scores(0, 0, hd, 0)
            st_ref[hd, 1] = jnp.ones((1, tq), F32)
            p_ref[1, hd] = jnp.zeros((tk, tq), BF16)
            acc_ref[hd] = jnp.zeros((V_ROWS, tq), F32)

    init = tuple((jnp.full((1, tq), -jnp.inf, F32), st_ref[hd, 1], st_ref[hd, 0]) for hd in range(2))
    fin = lax.fori_loop(0, jnp.where(i == nq, 1, nk), step, init)
    for hd in range(2):
        st_ref[hd, 0] = fin[hd][2]
        st_ref[hd, 1] = fin[hd][1]


def _mla(qt, k, vt):
    B, H, S, _ = k.shape
    nq, tq = qt.shape[2], qt.shape[4]
    nk, tk = vt.shape[2], vt.shape[4]
    resident = 2 * (2 * S * LANES + nk * V_ROWS * tk) * 2
    scratch = 2 * 2 * tk * tq * (4 + 2)
    once = dict(pipeline_mode=pl.Buffered(1)) if 2 * resident + scratch > VMEM_LIMIT * 3 // 4 else {}
    return pl.pallas_call(
        _mla_kernel,
        grid=(B, H // 2, nq + 1),
        in_specs=[pl.BlockSpec((1, 2, nq, LANES, tq), lambda b, p, i: (b, p, 0, 0, 0), **once),
                  pl.BlockSpec((1, 2, S, LANES), lambda b, p, i: (b, p, 0, 0), **once),
                  pl.BlockSpec((1, 2, nk, V_ROWS, tk), lambda b, p, i: (b, p, 0, 0, 0), **once)],
        out_specs=pl.BlockSpec((1, tq, LANES), lambda b, p, i: (b, jnp.maximum(i - 1, 0), p)),
        out_shape=jax.ShapeDtypeStruct((B, S, MLA_WIDTH), BF16),
        scratch_shapes=[pltpu.VMEM((2, 2, tk, tq), F32), pltpu.VMEM((2, 2, tk, tq), BF16),
                        pltpu.VMEM((2, V_ROWS, tq), F32), pltpu.VMEM((2, 2, 1, tq), F32)],
        compiler_params=pltpu.CompilerParams(
            dimension_semantics=("parallel", "parallel", "arbitrary"), vmem_limit_bytes=VMEM_LIMIT),
        name="mla_flash",
    )(qt, k, vt)


NA_QROWS = 4
NA_BROWS = NA_QROWS + NA_WIN_R
NA_QTOK = NA_QROWS * GRID_W
NA_BTOK = NA_BROWS * GRID_W
NA_STRIP = 32
NA_TILES = NA_BROWS // NA_QROWS


def _na_kernel(qt_ref, k_ref, vt_ref, tab_ref, o_ref, s_ref, p_ref, *, nblocks, nblk):
    i = pl.program_id(2)

    def band(j):
        g = i * nblk + j
        t0 = jnp.clip(g - NA_WIN_R // 2 // NA_QROWS, 0, nblocks - NA_TILES)
        var = jnp.where(g == 0, 0, jnp.where(g == nblocks - 1, 2, 1))
        return t0, var

    def scores(j, slot):
        t0, var = band(j)
        kb = k_ref[0, 0, pl.ds(pl.multiple_of(t0 * NA_QTOK, NA_QTOK), NA_BTOK), :]
        mxs = []
        for hd in range(2):
            st = _dot(kb, qt_ref[0, hd, j]) + tab_ref[var, hd]
            s_ref[slot, hd] = st
            mx = st[0:NA_STRIP]
            for r in range(NA_STRIP, NA_BTOK, NA_STRIP):
                mx = jnp.maximum(mx, st[r:r + NA_STRIP])
            mxs.append(jnp.max(mx, axis=0, keepdims=True))
        return tuple(mxs)

    def numerators(slot, mxs):
        for hd in range(2):
            for r in range(0, NA_BTOK, NA_STRIP):
                st = s_ref[slot, hd, r:r + NA_STRIP, :]
                p_ref[slot, hd, r:r + NA_STRIP, :] = jnp.exp2(st - mxs[hd]).astype(BF16)

    def values(j, slot):
        t0, _ = band(j)
        outs = []
        for hd in range(2):
            acc = _dot(vt_ref[0, hd, t0], p_ref[slot, hd, 0:NA_QTOK, :])
            for t in range(1, NA_TILES):
                acc = acc + _dot(vt_ref[0, hd, t0 + t], p_ref[slot, hd, t * NA_QTOK:(t + 1) * NA_QTOK, :])
            outs.append(acc[:NA_DIM] / acc[NA_DIM:NA_DIM + 1])
        rows = pl.ds(pl.multiple_of(j * NA_QTOK, NA_QTOK), NA_QTOK)
        o_ref[0, rows, :] = jnp.concatenate(outs, axis=0).T.astype(o_ref.dtype)

    def block(j, cur, mxs):
        mxs_next = scores(jnp.minimum(j + 1, nblk - 1), 1 - cur)
        numerators(cur, mxs)
        values(jnp.maximum(j - 1, 0), 1 - cur)
        return mxs_next

    def step(j, mxs):
        return lax.cond(j % 2 == 0, lambda: block(j, 0, mxs), lambda: block(j, 1, mxs))

    mxs0 = scores(0, 0)
    p_ref[1] = jnp.ones(p_ref.shape[1:], BF16)
    lax.fori_loop(0, nblk, step, mxs0)
    values(nblk - 1, (nblk - 1) % 2)


def _natten(qt, k, vt, tab, nblk):
    B, HP, S, _ = k.shape
    nblocks = S // NA_QTOK
    assert nblocks >= NA_TILES and nblocks % nblk == 0 and nblk >= 2
    kern = functools.partial(_na_kernel, nblocks=nblocks, nblk=nblk)
    return pl.pallas_call(
        kern,
        grid=(HP, B, nblocks // nblk),
        in_specs=[pl.BlockSpec((1, 2, nblk, LANES, NA_QTOK), lambda p, b, i: (b, p, i, 0, 0)),
                  pl.BlockSpec((1, 1, S, LANES), lambda p, b, i: (b, p, 0, 0)),
                  pl.BlockSpec((1, 2, nblocks, V_ROWS, NA_QTOK), lambda p, b, i: (b, p, 0, 0, 0)),
                  pl.BlockSpec((3, 2, NA_BTOK, NA_QTOK), lambda p, b, i: (0, p, 0, 0))],
        out_specs=pl.BlockSpec((1, nblk * NA_QTOK, LANES), lambda p, b, i: (b, i, p)),
        out_shape=jax.ShapeDtypeStruct((B, S, NA_WIDTH), BF16),
        scratch_shapes=[pltpu.VMEM((2, 2, NA_BTOK, NA_QTOK), F32), pltpu.VMEM((2, 2, NA_BTOK, NA_QTOK), BF16)],
        compiler_params=pltpu.CompilerParams(
            dimension_semantics=("parallel", "parallel", "arbitrary"), vmem_limit_bytes=VMEM_LIMIT),
        name="natten",
    )(qt, k, vt, tab)


def _oproj_kernel(a_ref, n_ref, x_ref, wa_ref, wn_ref, g_ref, o_ref):
    mix = _dot(a_ref[0], wa_ref[...]) + _dot(n_ref[0], wn_ref[...])
    o_ref[0] = x_ref[0] + _rms(mix, g_ref[...])


def _oproj(a, n, x, wts, tm):
    B, S, _ = x.shape
    tok = lambda w: pl.BlockSpec((1, tm, w), lambda b, i: (b, i, 0))
    consts = [wts["w_o_a"], wts["w_o_n"], wts["g_mix_post"]]
    return pl.pallas_call(
        _oproj_kernel,
        grid=(B, S // tm),
        in_specs=[tok(MLA_WIDTH), tok(NA_WIDTH), tok(D_MODEL)] + [_const_spec(c.shape) for c in consts],
        out_specs=tok(D_MODEL),
        out_shape=jax.ShapeDtypeStruct((B, S, D_MODEL), F32),
        compiler_params=pltpu.CompilerParams(
            dimension_semantics=("parallel", "parallel"), vmem_limit_bytes=VMEM_LIMIT),
        name="out_proj",
    )(a, n, x, *consts)


def _gelu_tanh(x):
    return 0.5 * x * (1.0 + jnp.tanh(0.7978845608028654 * (x + 0.044715 * (x * x * x))))


def _ffn_kernel(x_ref, prev_ref, next_ref, g_pre_ref, w_up_ref, cw_ref, cb_ref, w_dn_ref, g_post_ref, o_ref, act_ref,
                *, fc):
    i = pl.program_id(1)
    last = pl.num_programs(1) - 1
    x = x_ref[0]
    tm = x.shape[0]
    prev = jnp.where(i > 0, prev_ref[0], 0.0)
    nxt = jnp.where(i < last, next_ref[0], 0.0)
    xh = jnp.concatenate([prev, x, nxt], axis=0)
    hn = _rms(xh, g_pre_ref[...]).astype(BF16)
    n_ext = tm + 2 * HALO

    def conv(hx, col):
        w = cw_ref[:, col:col + fc]
        left = pltpu.roll(hx, 1, 0)[HALO:HALO + tm]
        right = pltpu.roll(hx, n_ext - 1, 0)[HALO:HALO + tm]
        return cb_ref[:, col:col + fc] + left * w[0:1] + hx[HALO:HALO + tm] * w[1:2] + right * w[2:3]

    for c in range(D_FF // fc):
        cg = c * fc
        cu = D_FF + c * fc
        g = conv(_dot(hn, w_up_ref[:, cg:cg + fc]), cg)
        u = conv(_dot(hn, w_up_ref[:, cu:cu + fc]), cu)
        act_ref[:, cg:cg + fc] = (_gelu_tanh(g) * u).astype(BF16)
    o_ref[0] = x + _rms(_dot(act_ref[...], w_dn_ref[...]), g_post_ref[...])


def _ffn(x, wts, tm, fc):
    B, S, _ = x.shape
    nb = tm // HALO
    kern = functools.partial(_ffn_kernel, fc=fc)
    consts = [wts["g_ffn_pre"], wts["w_ffn_up"], wts["ffn_conv_w"], wts["ffn_conv_b"], wts["w_ffn_down"],
              wts["g_ffn_post"]]
    return pl.pallas_call(
        kern,
        grid=(B, S // tm),
        in_specs=[pl.BlockSpec((1, tm, D_MODEL), lambda b, i: (b, i, 0)),
                  pl.BlockSpec((1, HALO, D_MODEL), lambda b, i: (b, jnp.maximum(i * nb - 1, 0), 0)),
                  pl.BlockSpec((1, HALO, D_MODEL), lambda b, i: (b, jnp.minimum((i + 1) * nb, S // HALO - 1), 0))]
        + [pl.BlockSpec(c.shape, lambda b, i, nd=c.ndim: (0,) * nd, pipeline_mode=pl.Buffered(1)) for c in consts],
        out_specs=pl.BlockSpec((1, tm, D_MODEL), lambda b, i: (b, i, 0)),
        out_shape=jax.ShapeDtypeStruct((B, S, D_MODEL), F32),
        scratch_shapes=[pltpu.VMEM((tm, D_FF), BF16)],
        compiler_params=pltpu.CompilerParams(
            dimension_semantics=("parallel", "parallel"), vmem_limit_bytes=VMEM_LIMIT),
        name="conv_ffn",
    )(x, x, x, *consts)


def _prep_weights(g_mix_pre, w_in, g_q_lat, w_q_up, g_kv_lat, w_kv_up, w_o, g_mix_post,
                  g_ffn_pre, w_ffn_up, ffn_conv_w, ffn_conv_b, w_ffn_down, g_ffn_post):
    half = QK_ROPE // 2
    o1, o2, o3 = Q_LORA, Q_LORA + KV_LORA, Q_LORA + KV_LORA + QK_ROPE
    row = lambda g: g.reshape(1, -1).astype(F32)

    w_kr = w_in[:, o2:o3]
    zl = jnp.zeros((D_MODEL, QK_NOPE), F32)
    zr = jnp.zeros((D_MODEL, LANES - QK_NOPE - QK_ROPE), F32)
    w_kr_blk = jnp.concatenate([zl, w_kr, zr], axis=1)
    w_krrot_blk = jnp.concatenate([zl, -w_kr[:, half:], w_kr[:, :half], zr], axis=1)
    w_lat = jnp.concatenate([w_in[:, :o2], w_kr_blk, w_krrot_blk], axis=1)

    wq = w_q_up.reshape(Q_LORA, MLA_HEADS, QK_NOPE + QK_ROPE)
    zq = jnp.zeros((Q_LORA, MLA_HEADS, LANES - QK_NOPE - QK_ROPE), F32)
    w_q = jnp.concatenate([wq, zq], axis=2).reshape(Q_LORA, MLA_HEADS * LANES)

    wkv = w_kv_up.reshape(KV_LORA, MLA_HEADS, QK_NOPE + V_DIM)
    w_k = jnp.concatenate([wkv[:, :, :QK_NOPE], jnp.zeros((KV_LORA, MLA_HEADS, LANES - QK_NOPE), F32)],
                          axis=2).reshape(KV_LORA, MLA_HEADS * LANES)
    w_v = jnp.concatenate([wkv[:, :, QK_NOPE:], jnp.zeros((KV_LORA, MLA_HEADS, V_ROWS - V_DIM), F32)],
                          axis=2).reshape(KV_LORA, MLA_HEADS * V_ROWS)

    w_nv = jnp.concatenate([w_in[:, o3 + 2 * NA_WIDTH:].reshape(D_MODEL, NA_HEADS, NA_DIM),
                            jnp.zeros((D_MODEL, NA_HEADS, V_ROWS - NA_DIM), F32)],
                           axis=2).reshape(D_MODEL, NA_HEADS * V_ROWS)

    return {
        "g_mix_pre": row(g_mix_pre), "w_lat": w_lat.astype(BF16),
        "w_nqt": w_in[:, o3:o3 + NA_WIDTH].T.astype(BF16),
        "w_nk": w_in[:, o3 + NA_WIDTH:o3 + 2 * NA_WIDTH].astype(BF16),
        "w_nvt": w_nv.T.astype(BF16),
        "g_q_lat": row(g_q_lat), "w_qt": w_q.T.astype(BF16),
        "g_kv_lat": row(g_kv_lat), "w_k": w_k.astype(BF16), "w_vt": w_v.T.astype(BF16),
        "w_o_a": w_o[:MLA_WIDTH].astype(BF16), "w_o_n": w_o[MLA_WIDTH:].astype(BF16), "g_mix_post": row(g_mix_post),
        "g_ffn_pre": row(g_ffn_pre), "w_ffn_up": w_ffn_up.astype(BF16), "ffn_conv_w": ffn_conv_w.astype(F32),
        "ffn_conv_b": row(ffn_conv_b), "w_ffn_down": w_ffn_down.astype(BF16), "g_ffn_post": row(g_ffn_post),
    }


def _rope_tables(S):
    inv = 1.0 / (ROPE_THETA ** (jnp.arange(0, QK_ROPE, 2, dtype=F32) / QK_ROPE))
    ang = jnp.arange(S, dtype=F32)[:, None] * inv[None, :]
    cos, sin = jnp.cos(ang), jnp.sin(ang)
    pad = jnp.zeros((S, LANES - QK_NOPE - QK_ROPE), F32)
    cos_tab = jnp.concatenate([jnp.ones((S, QK_NOPE), F32), cos, cos, pad], axis=1)
    sin_tab = jnp.concatenate([jnp.zeros((S, QK_NOPE), F32), sin, sin, pad], axis=1)
    return cos_tab, sin_tab, cos_tab.T, sin_tab.T


def _na_tables(rpb):
    qc = jnp.arange(GRID_W)[:, None]
    kc = jnp.arange(GRID_W)[None, :]
    dc = jnp.clip(kc - qc + (NA_WIN_C - 1), 0, 2 * NA_WIN_C - 2)
    qs = jnp.clip(qc - NA_WIN_C // 2, 0, GRID_W - NA_WIN_C)
    col_valid = (kc >= qs) & (kc < qs + NA_WIN_C)
    rpb = rpb.astype(F32)
    tile = jnp.zeros(rpb.shape[:2] + (GRID_W, GRID_W), F32)
    for e in range(2 * NA_WIN_C - 1):
        tile = jnp.where(dc == e, rpb[:, :, e][:, :, None, None], tile)
    tile = jnp.where(col_valid, tile * LOG2_E, NEG_INF)
    outside = jnp.full((NA_HEADS, GRID_W, GRID_W), NEG_INF, F32)
    variants = []
    for shift, lo in ((0, lambda j: 0), (-NA_WIN_R // 2, lambda j: j), (-NA_WIN_R, lambda j: NA_BROWS - NA_WIN_R)):
        qrows = []
        for j in range(NA_QROWS):
            blocks = [tile[:, i - j + shift + NA_WIN_R - 1] if lo(j) <= i < lo(j) + NA_WIN_R else outside
                      for i in range(NA_BROWS)]
            qrows.append(jnp.concatenate(blocks, axis=2))
        variants.append(jnp.concatenate(qrows, axis=1))
    return jnp.swapaxes(jnp.stack(variants), 2, 3)


def _layer(x, wts, na_tab):
    B, S, _ = x.shape
    tm = 512
    qt, k, vt, nqt, nk, nvt = _proj(x, wts, _rope_tables(S), tm, tk=min(2048, S // 2))
    a = _mla(qt, k, vt)
    n = _natten(nqt, nk, nvt, na_tab, nblk=min(32, S // NA_QTOK))
    x1 = _oproj(a, n, x, wts, tm)
    return _ffn(x1, wts, tm, fc=256)


def kernel(x_prompt, x_sample, g_mix_pre, w_in, g_q_lat, w_q_up, g_kv_lat, w_kv_up, na_rpb, w_o, g_mix_post,
           g_ffn_pre, w_ffn_up, ffn_conv_w, ffn_conv_b, w_ffn_down, g_ffn_post):
    y_prompt, y_sample = x_prompt, x_sample
    for l in range(g_mix_pre.shape[0]):
        wts = _prep_weights(g_mix_pre[l], w_in[l], g_q_lat[l], w_q_up[l], g_kv_lat[l], w_kv_up[l], w_o[l],
                            g_mix_post[l], g_ffn_pre[l], w_ffn_up[l], ffn_conv_w[l], ffn_conv_b[l],
                            w_ffn_down[l], g_ffn_post[l])
        na_tab = _na_tables(na_rpb[l])
        y_prompt = _layer(y_prompt, wts, na_tab)
        y_sample = _layer(y_sample, wts, na_tab)
    return (y_prompt, y_sample)
```

```python
import functools

import jax
import jax.numpy as jnp
from jax import lax
from jax.experimental import pallas as pl
from jax.experimental.pallas import tpu as pltpu

D_MODEL = 1024
GRID_W = 64
MLA_HEADS = 8
Q_LORA = 256
KV_LORA = 128
QK_NOPE = 64
QK_ROPE = 32
V_DIM = 64
ROPE_THETA = 10000.0
NA_HEADS = 8
NA_DIM = 64
NA_WIN_R = 8
NA_WIN_C = 16
NA_WIDTH = NA_HEADS * NA_DIM
MLA_WIDTH = MLA_HEADS * V_DIM
D_FF = 2816
EPS = 1e-6
NEG_INF = -1e30

LANES = 128
V_ROWS = 80
LOG2_E = 1.4426950408889634
Q_SCALE = (QK_NOPE + QK_ROPE) ** -0.5 * LOG2_E
NA_Q_SCALE = NA_DIM ** -0.5 * LOG2_E
HALO = 8
VMEM_LIMIT = 56 * 1024 * 1024

BF16 = jnp.bfloat16
F32 = jnp.float32


def _rms(x, g):
    return x * lax.rsqrt(jnp.mean(x * x, axis=-1, keepdims=True) + EPS) * g


def _dot(a, b):
    return jnp.dot(a, b, preferred_element_type=F32)


def _const_spec(shape):
    return pl.BlockSpec(shape, lambda *_: (0,) * len(shape))


def _proj_kernel(x_ref, g_pre_ref, w_lat_ref, w_nqt_ref, w_nk_ref, w_nvt_ref, g_q_ref, w_qt_ref,
                 g_kv_ref, w_k_ref, w_vt_ref, cos_ref, sin_ref, cost_ref, sint_ref,
                 qt_ref, k_ref, vt_ref, nqt_ref, nk_ref, nvt_ref):
    hn = _rms(x_ref[0], g_pre_ref[...])
    h = hn.astype(BF16)
    tm = h.shape[0]
    cos = cos_ref[...]
    sin = sin_ref[...]
    cost = cost_ref[...]
    sint = sint_ref[...]

    z = _dot(h, w_lat_ref[...])
    c_q = z[:, :Q_LORA]
    c_kv = z[:, Q_LORA:Q_LORA + KV_LORA]
    k_r = z[:, Q_LORA + KV_LORA:Q_LORA + KV_LORA + LANES]
    k_rrot = z[:, Q_LORA + KV_LORA + LANES:]
    k_pe = k_r * cos + k_rrot * sin

    cqn = _rms(c_q, g_q_ref[...])
    cqnt = cqn.T.astype(BF16)
    qmt = _dot(w_qt_ref[...], cqnt)
    ckvn = _rms(c_kv, g_kv_ref[...])
    kn = _dot(ckvn.astype(BF16), w_k_ref[...])
    vvt = _dot(w_vt_ref[...], ckvn.T.astype(BF16))
    ones_row = lax.broadcasted_iota(jnp.int32, (V_ROWS, tm), 0) == V_DIM
    half = QK_ROPE // 2
    for hd in range(MLA_HEADS):
        sl = slice(hd * LANES, (hd + 1) * LANES)
        qh = qmt[sl]
        qrot = jnp.concatenate([qh[:QK_NOPE], -qh[QK_NOPE + half:QK_NOPE + QK_ROPE], qh[QK_NOPE:QK_NOPE + half],
                                qh[QK_NOPE + QK_ROPE:]], axis=0)
        qt_ref[0, hd, 0] = ((qh * cost + qrot * sint) * Q_SCALE).astype(BF16)
        k_ref[0, hd] = (kn[:, sl] + k_pe).astype(BF16)
        vt_ref[0, hd, 0] = jnp.where(ones_row, 1.0, vvt[hd * V_ROWS:(hd + 1) * V_ROWS]).astype(BF16)

    ht = hn.T.astype(BF16)
    nk = _dot(h, w_nk_ref[...])
    nqt = _dot(w_nqt_ref[...], ht) * NA_Q_SCALE
    nvt = _dot(w_nvt_ref[...], ht)
    low_rows = lax.broadcasted_iota(jnp.int32, (LANES, NA_QTOK), 0) < NA_DIM
    na_ones_row = lax.broadcasted_iota(jnp.int32, (V_ROWS, NA_QTOK), 0) == NA_DIM
    for p in range(NA_HEADS // 2):
        nk_ref[0, p] = nk[:, p * LANES:(p + 1) * LANES].astype(BF16)
        for t in range(tm // NA_QTOK):
            cols = slice(t * NA_QTOK, (t + 1) * NA_QTOK)
            pair = nqt[p * LANES:(p + 1) * LANES, cols]
            nqt_ref[0, 2 * p, t] = jnp.where(low_rows, pair, 0.0).astype(BF16)
            nqt_ref[0, 2 * p + 1, t] = jnp.where(low_rows, 0.0, pair).astype(BF16)
            for hd in (2 * p, 2 * p + 1):
                nvt_ref[0, hd, t] = jnp.where(na_ones_row, 1.0, nvt[hd * V_ROWS:(hd + 1) * V_ROWS, cols]).astype(BF16)


def _proj(x, wts, rope, tm, tk):
    B, S, _ = x.shape
    nt = S // tm
    per = tk // tm
    nqt = tm // NA_QTOK
    hp = NA_HEADS // 2
    head_out = lambda n: pl.BlockSpec((1, n, tm, LANES), lambda b, i: (b, 0, i, 0))
    names = ("g_mix_pre", "w_lat", "w_nqt", "w_nk", "w_nvt", "g_q_lat", "w_qt", "g_kv_lat", "w_k", "w_vt")
    consts = [wts[n] for n in names]
    cos_tab, sin_tab, cos_t, sin_t = rope
    return pl.pallas_call(
        _proj_kernel,
        grid=(B, nt),
        in_specs=[pl.BlockSpec((1, tm, D_MODEL), lambda b, i: (b, i, 0))]
        + [_const_spec(c.shape) for c in consts]
        + [pl.BlockSpec((tm, LANES), lambda b, i: (i, 0))] * 2
        + [pl.BlockSpec((LANES, tm), lambda b, i: (0, i))] * 2,
        out_specs=[pl.BlockSpec((1, MLA_HEADS, 1, LANES, tm), lambda b, i: (b, 0, i, 0, 0)),
                   head_out(MLA_HEADS),
                   pl.BlockSpec((1, MLA_HEADS, 1, V_ROWS, tm), lambda b, i: (b, 0, i // per, 0, i % per)),
                   pl.BlockSpec((1, NA_HEADS, nqt, LANES, NA_QTOK), lambda b, i: (b, 0, i, 0, 0)),
                   head_out(hp),
                   pl.BlockSpec((1, NA_HEADS, nqt, V_ROWS, NA_QTOK), lambda b, i: (b, 0, i, 0, 0))],
        out_shape=[jax.ShapeDtypeStruct((B, MLA_HEADS, nt, LANES, tm), BF16),
                   jax.ShapeDtypeStruct((B, MLA_HEADS, S, LANES), BF16),
                   jax.ShapeDtypeStruct((B, MLA_HEADS, S // tk, V_ROWS, tk), BF16),
                   jax.ShapeDtypeStruct((B, NA_HEADS, S // NA_QTOK, LANES, NA_QTOK), BF16),
                   jax.ShapeDtypeStruct((B, hp, S, LANES), BF16),
                   jax.ShapeDtypeStruct((B, NA_HEADS, S // NA_QTOK, V_ROWS, NA_QTOK), BF16)],
        compiler_params=pltpu.CompilerParams(
            dimension_semantics=("parallel", "parallel"), vmem_limit_bytes=VMEM_LIMIT),
        name="proj",
    )(x, *consts, cos_tab, sin_tab, cos_t, sin_t)


MLA_STRIP = 32
MLA_SPLIT = 2 * LANES


def _mla_kernel(qt_ref, k_ref, vt_ref, o_ref, s_ref, p_ref, acc_ref, st_ref):
    nq, tq = qt_ref.shape[2], qt_ref.shape[4]
    nk, tk = vt_ref.shape[2], vt_ref.shape[4]
    assert nk % 2 == 0 and tk % MLA_STRIP == 0 and tq % MLA_SPLIT == 0
    i = pl.program_id(2)
    qi = jnp.minimum(i, nq - 1)

    def slot_lanes(slot):
        return [slice((2 * h2 + slot) * MLA_SPLIT, (2 * h2 + slot + 1) * MLA_SPLIT) for h2 in range(tq // MLA_SPLIT)]

    def scores(qj, c, hd, slot):
        ks = pl.ds(pl.multiple_of(c * tk, tk), tk)
        st = _dot(k_ref[0, hd, ks, :], qt_ref[0, hd, qj])
        for h2, lanes in enumerate(slot_lanes(slot)):
            s_ref[hd, :, lanes] = st[:, h2 * MLA_SPLIT:(h2 + 1) * MLA_SPLIT]
        mx = st[0:MLA_STRIP]
        for r in range(MLA_STRIP, tk, MLA_STRIP):
            mx = jnp.maximum(mx, st[r:r + MLA_STRIP])
        return jnp.max(mx, axis=0, keepdims=True)

    def numerators(hd, slot, m, mx):
        m_new = jnp.maximum(m, mx)
        alpha = jnp.exp2(m - m_new)
        for r in range(0, tk, MLA_STRIP):
            for h2, (s_lanes, p_lanes) in enumerate(zip(slot_lanes(slot), slot_lanes(1 - slot))):
                st = s_ref[hd, r:r + MLA_STRIP, s_lanes]
                mh = m_new[:, h2 * MLA_SPLIT:(h2 + 1) * MLA_SPLIT]
                p_ref[hd, r:r + MLA_STRIP, p_lanes] = jnp.exp2(st - mh).astype(BF16)
        return m_new, alpha

    def accumulate(c, hd, slot, alpha):
        pt = jnp.concatenate([p_ref[hd, :, lanes] for lanes in slot_lanes(1 - slot)], axis=1)
        acc_ref[hd] = alpha * acc_ref[hd] + _dot(vt_ref[0, hd, c], pt)

    def chunk(c, cur, carry):
        nxt = 1 - cur
        wraps = c + 1 == nk
        qj_next = jnp.where(wraps, jnp.minimum(i + 1, nq - 1), qi)
        c_next = jnp.where(wraps, 0, c + 1)
        c_prev = jnp.where(c == 0, nk - 1, c - 1)
        out = []
        for hd in range(2):
            m, alpha_prev, mx = carry[hd]
            mx_next = scores(qj_next, c_next, hd, nxt)
            m_new, alpha = numerators(hd, cur, m, mx)
            accumulate(c_prev, hd, nxt, alpha_prev)
            out.append((m_new, alpha, mx_next))
        return tuple(out)

    def step(c, carry):
        carry = lax.cond(c % 2 == 0, lambda: chunk(c, 0, carry), lambda: chunk(c, 1, carry))

        @pl.when((c == 0) & (i > 0))
        def _():
            outs = []
            for hd in range(2):
                acc = acc_ref[hd]
                outs.append(acc[:V_DIM] / acc[V_DIM:V_DIM + 1])
            o_ref[0] = jnp.concatenate(outs, axis=0).T.astype(o_ref.dtype)

        return carry

    @pl.when(i == 0)
    def _():
        for hd in range(2):
            st_ref[hd, 0] = scores(0, 0, hd, 0)
            st_ref[hd, 1] = jnp.ones((1, tq), F32)
            for lanes in slot_lanes(0):
                p_ref[hd, :, lanes] = jnp.zeros((tk, MLA_SPLIT), BF16)
            acc_ref[hd] = jnp.zeros((V_ROWS, tq), F32)

    init = tuple((jnp.full((1, tq), -jnp.inf, F32), st_ref[hd, 1], st_ref[hd, 0]) for hd in range(2))
    fin = lax.fori_loop(0, jnp.where(i == nq, 1, nk), step, init)
    for hd in range(2):
        st_ref[hd, 0] = fin[hd][2]
        st_ref[hd, 1] = fin[hd][1]


def _mla(qt, k, vt):
    B, H, S, _ = k.shape
    nq, tq = qt.shape[2], qt.shape[4]
    nk, tk = vt.shape[2], vt.shape[4]
    resident = 2 * (2 * S * LANES + nk * V_ROWS * tk) * 2
    scratch = 2 * 2 * tk * tq * (4 + 2)
    once = dict(pipeline_mode=pl.Buffered(1)) if 2 * resident + scratch > VMEM_LIMIT * 3 // 4 else {}
    return pl.pallas_call(
        _mla_kernel,
        grid=(B, H // 2, nq + 1),
        in_specs=[pl.BlockSpec((1, 2, nq, LANES, tq), lambda b, p, i: (b, p, 0, 0, 0), **once),
                  pl.BlockSpec((1, 2, S, LANES), lambda b, p, i: (b, p, 0, 0), **once),
                  pl.BlockSpec((1, 2, nk, V_ROWS, tk), lambda b, p, i: (b, p, 0, 0, 0), **once)],
        out_specs=pl.BlockSpec((1, tq, LANES), lambda b, p, i: (b, jnp.maximum(i - 1, 0), p)),
        out_shape=jax.ShapeDtypeStruct((B, S, MLA_WIDTH), BF16),
        scratch_shapes=[pltpu.VMEM((2, tk, 2 * tq), F32), pltpu.VMEM((2, tk, 2 * tq), BF16),
                        pltpu.VMEM((2, V_ROWS, tq), F32), pltpu.VMEM((2, 2, 1, tq), F32)],
        compiler_params=pltpu.CompilerParams(
            dimension_semantics=("parallel", "parallel", "arbitrary"), vmem_limit_bytes=VMEM_LIMIT),
        name="mla_flash",
    )(qt, k, vt)


NA_QROWS = 4
NA_BROWS = NA_QROWS + NA_WIN_R
NA_QTOK = NA_QROWS * GRID_W
NA_BTOK = NA_BROWS * GRID_W
NA_STRIP = 32
NA_TILES = NA_BROWS // NA_QROWS


def _na_kernel(qt_ref, k_ref, vt_ref, tab_ref, o_ref, s_ref, p_ref, *, nblocks, nblk):
    i = pl.program_id(2)

    def band(j):
        g = i * nblk + j
        t0 = jnp.clip(g - NA_WIN_R // 2 // NA_QROWS, 0, nblocks - NA_TILES)
        var = jnp.where(g == 0, 0, jnp.where(g == nblocks - 1, 2, 1))
        return t0, var

    def scores(j, slot):
        t0, var = band(j)
        kb = k_ref[0, 0, pl.ds(pl.multiple_of(t0 * NA_QTOK, NA_QTOK), NA_BTOK), :]
        mxs = []
        for hd in range(2):
            st = _dot(kb, qt_ref[0, hd, j]) + tab_ref[var, hd]
            s_ref[slot, hd] = st
            mx = st[0:NA_STRIP]
            for r in range(NA_STRIP, NA_BTOK, NA_STRIP):
                mx = jnp.maximum(mx, st[r:r + NA_STRIP])
            mxs.append(jnp.max(mx, axis=0, keepdims=True))
        return tuple(mxs)

    def numerators(slot, mxs):
        for hd in range(2):
            for r in range(0, NA_BTOK, NA_STRIP):
                st = s_ref[slot, hd, r:r + NA_STRIP, :]
                p_ref[slot, hd, r:r + NA_STRIP, :] = jnp.exp2(st - mxs[hd]).astype(BF16)

    def values(j, slot):
        t0, _ = band(j)
        outs = []
        for hd in range(2):
            acc = _dot(vt_ref[0, hd, t0], p_ref[slot, hd, 0:NA_QTOK, :])
            for t in range(1, NA_TILES):
                acc = acc + _dot(vt_ref[0, hd, t0 + t], p_ref[slot, hd, t * NA_QTOK:(t + 1) * NA_QTOK, :])
            outs.append(acc[:NA_DIM] / acc[NA_DIM:NA_DIM + 1])
        rows = pl.ds(pl.multiple_of(j * NA_QTOK, NA_QTOK), NA_QTOK)
        o_ref[0, rows, :] = jnp.concatenate(outs, axis=0).T.astype(o_ref.dtype)

    def block(j, cur, mxs):
        mxs_next = scores(jnp.minimum(j + 1, nblk - 1), 1 - cur)
        numerators(cur, mxs)
        values(jnp.maximum(j - 1, 0), 1 - cur)
        return mxs_next

    def step(j, mxs):
        return lax.cond(j % 2 == 0, lambda: block(j, 0, mxs), lambda: block(j, 1, mxs))

    mxs0 = scores(0, 0)
    p_ref[1] = jnp.ones(p_ref.shape[1:], BF16)
    lax.fori_loop(0, nblk, step, mxs0)
    values(nblk - 1, (nblk - 1) % 2)


def _natten(qt, k, vt, tab, nblk):
    B, HP, S, _ = k.shape
    nblocks = S // NA_QTOK
    assert nblocks >= NA_TILES and nblocks % nblk == 0 and nblk >= 2
    kern = functools.partial(_na_kernel, nblocks=nblocks, nblk=nblk)
    return pl.pallas_call(
        kern,
        grid=(HP, B, nblocks // nblk),
        in_specs=[pl.BlockSpec((1, 2, nblk, LANES, NA_QTOK), lambda p, b, i: (b, p, i, 0, 0)),
                  pl.BlockSpec((1, 1, S, LANES), lambda p, b, i: (b, p, 0, 0)),
                  pl.BlockSpec((1, 2, nblocks, V_ROWS, NA_QTOK), lambda p, b, i: (b, p, 0, 0, 0)),
                  pl.BlockSpec((3, 2, NA_BTOK, NA_QTOK), lambda p, b, i: (0, p, 0, 0))],
        out_specs=pl.BlockSpec((1, nblk * NA_QTOK, LANES), lambda p, b, i: (b, i, p)),
        out_shape=jax.ShapeDtypeStruct((B, S, NA_WIDTH), BF16),
        scratch_shapes=[pltpu.VMEM((2, 2, NA_BTOK, NA_QTOK), F32), pltpu.VMEM((2, 2, NA_BTOK, NA_QTOK), BF16)],
        compiler_params=pltpu.CompilerParams(
            dimension_semantics=("parallel", "parallel", "arbitrary"), vmem_limit_bytes=VMEM_LIMIT),
        name="natten",
    )(qt, k, vt, tab)


def _oproj_kernel(a_ref, n_ref, x_ref, wa_ref, wn_ref, g_ref, o_ref):
    mix = _dot(a_ref[0], wa_ref[...]) + _dot(n_ref[0], wn_ref[...])
    o_ref[0] = x_ref[0] + _rms(mix, g_ref[...])


def _oproj(a, n, x, wts, tm):
    B, S, _ = x.shape
    tok = lambda w: pl.BlockSpec((1, tm, w), lambda b, i: (b, i, 0))
    consts = [wts["w_o_a"], wts["w_o_n"], wts["g_mix_post"]]
    return pl.pallas_call(
        _oproj_kernel,
        grid=(B, S // tm),
        in_specs=[tok(MLA_WIDTH), tok(NA_WIDTH), tok(D_MODEL)] + [_const_spec(c.shape) for c in consts],
        out_specs=tok(D_MODEL),
        out_shape=jax.ShapeDtypeStruct((B, S, D_MODEL), F32),
        compiler_params=pltpu.CompilerParams(
            dimension_semantics=("parallel", "parallel"), vmem_limit_bytes=VMEM_LIMIT),
        name="out_proj",
    )(a, n, x, *consts)


def _gelu_tanh(x):
    return 0.5 * x * (1.0 + jnp.tanh(0.7978845608028654 * (x + 0.044715 * (x * x * x))))


def _ffn_kernel(x_ref, prev_ref, next_ref, g_pre_ref, w_up_ref, cw_ref, cb_ref, w_dn_ref, g_post_ref, o_ref, act_ref,
                *, fc):
    i = pl.program_id(1)
    last = pl.num_programs(1) - 1
    x = x_ref[0]
    tm = x.shape[0]
    prev = jnp.where(i > 0, prev_ref[0], 0.0)
    nxt = jnp.where(i < last, next_ref[0], 0.0)
    xh = jnp.concatenate([prev, x, nxt], axis=0)
    hn = _rms(xh, g_pre_ref[...]).astype(BF16)
    n_ext = tm + 2 * HALO

    def conv(hx, col):
        w = cw_ref[:, col:col + fc]
        left = pltpu.roll(hx, 1, 0)[HALO:HALO + tm]
        right = pltpu.roll(hx, n_ext - 1, 0)[HALO:HALO + tm]
        return cb_ref[:, col:col + fc] + left * w[0:1] + hx[HALO:HALO + tm] * w[1:2] + right * w[2:3]

    for c in range(D_FF // fc):
        cg = c * fc
        cu = D_FF + c * fc
        g = conv(_dot(hn, w_up_ref[:, cg:cg + fc]), cg)
        u = conv(_dot(hn, w_up_ref[:, cu:cu + fc]), cu)
        act_ref[:, cg:cg + fc] = (_gelu_tanh(g) * u).astype(BF16)
    o_ref[0] = x + _rms(_dot(act_ref[...], w_dn_ref[...]), g_post_ref[...])


def _ffn(x, wts, tm, fc):
    B, S, _ = x.shape
    nb = tm // HALO
    kern = functools.partial(_ffn_kernel, fc=fc)
    consts = [wts["g_ffn_pre"], wts["w_ffn_up"], wts["ffn_conv_w"], wts["ffn_conv_b"], wts["w_ffn_down"],
              wts["g_ffn_post"]]
    return pl.pallas_call(
        kern,
        grid=(B, S // tm),
        in_specs=[pl.BlockSpec((1, tm, D_MODEL), lambda b, i: (b, i, 0)),
                  pl.BlockSpec((1, HALO, D_MODEL), lambda b, i: (b, jnp.maximum(i * nb - 1, 0), 0)),
                  pl.BlockSpec((1, HALO, D_MODEL), lambda b, i: (b, jnp.minimum((i + 1) * nb, S // HALO - 1), 0))]
        + [pl.BlockSpec(c.shape, lambda b, i, nd=c.ndim: (0,) * nd, pipeline_mode=pl.Buffered(1)) for c in consts],
        out_specs=pl.BlockSpec((1, tm, D_MODEL), lambda b, i: (b, i, 0)),
        out_shape=jax.ShapeDtypeStruct((B, S, D_MODEL), F32),
        scratch_shapes=[pltpu.VMEM((tm, D_FF), BF16)],
        compiler_params=pltpu.CompilerParams(
            dimension_semantics=("parallel", "parallel"), vmem_limit_bytes=VMEM_LIMIT),
        name="conv_ffn",
    )(x, x, x, *consts)


def _prep_weights(g_mix_pre, w_in, g_q_lat, w_q_up, g_kv_lat, w_kv_up, w_o, g_mix_post,
                  g_ffn_pre, w_ffn_up, ffn_conv_w, ffn_conv_b, w_ffn_down, g_ffn_post):
    half = QK_ROPE // 2
    o1, o2, o3 = Q_LORA, Q_LORA + KV_LORA, Q_LORA + KV_LORA + QK_ROPE
    row = lambda g: g.reshape(1, -1).astype(F32)

    w_kr = w_in[:, o2:o3]
    zl = jnp.zeros((D_MODEL, QK_NOPE), F32)
    zr = jnp.zeros((D_MODEL, LANES - QK_NOPE - QK_ROPE), F32)
    w_kr_blk = jnp.concatenate([zl, w_kr, zr], axis=1)
    w_krrot_blk = jnp.concatenate([zl, -w_kr[:, half:], w_kr[:, :half], zr], axis=1)
    w_lat = jnp.concatenate([w_in[:, :o2], w_kr_blk, w_krrot_blk], axis=1)

    wq = w_q_up.reshape(Q_LORA, MLA_HEADS, QK_NOPE + QK_ROPE)
    zq = jnp.zeros((Q_LORA, MLA_HEADS, LANES - QK_NOPE - QK_ROPE), F32)
    w_q = jnp.concatenate([wq, zq], axis=2).reshape(Q_LORA, MLA_HEADS * LANES)

    wkv = w_kv_up.reshape(KV_LORA, MLA_HEADS, QK_NOPE + V_DIM)
    w_k = jnp.concatenate([wkv[:, :, :QK_NOPE], jnp.zeros((KV_LORA, MLA_HEADS, LANES - QK_NOPE), F32)],
                          axis=2).reshape(KV_LORA, MLA_HEADS * LANES)
    w_v = jnp.concatenate([wkv[:, :, QK_NOPE:], jnp.zeros((KV_LORA, MLA_HEADS, V_ROWS - V_DIM), F32)],
                          axis=2).reshape(KV_LORA, MLA_HEADS * V_ROWS)

    w_nv = jnp.concatenate([w_in[:, o3 + 2 * NA_WIDTH:].reshape(D_MODEL, NA_HEADS, NA_DIM),
                            jnp.zeros((D_MODEL, NA_HEADS, V_ROWS - NA_DIM), F32)],
                           axis=2).reshape(D_MODEL, NA_HEADS * V_ROWS)

    return {
        "g_mix_pre": row(g_mix_pre), "w_lat": w_lat.astype(BF16),
        "w_nqt": w_in[:, o3:o3 + NA_WIDTH].T.astype(BF16),
        "w_nk": w_in[:, o3 + NA_WIDTH:o3 + 2 * NA_WIDTH].astype(BF16),
        "w_nvt": w_nv.T.astype(BF16),
        "g_q_lat": row(g_q_lat), "w_qt": w_q.T.astype(BF16),
        "g_kv_lat": row(g_kv_lat), "w_k": w_k.astype(BF16), "w_vt": w_v.T.astype(BF16),
        "w_o_a": w_o[:MLA_WIDTH].astype(BF16), "w_o_n": w_o[MLA_WIDTH:].astype(BF16), "g_mix_post": row(g_mix_post),
        "g_ffn_pre": row(g_ffn_pre), "w_ffn_up": w_ffn_up.astype(BF16), "ffn_conv_w": ffn_conv_w.astype(F32),
        "ffn_conv_b": row(ffn_conv_b), "w_ffn_down": w_ffn_down.astype(BF16), "g_ffn_post": row(g_ffn_post),
    }


def _rope_tables(S):
    inv = 1.0 / (ROPE_THETA ** (jnp.arange(0, QK_ROPE, 2, dtype=F32) / QK_ROPE))
    ang = jnp.arange(S, dtype=F32)[:, None] * inv[None, :]
    cos, sin = jnp.cos(ang), jnp.sin(ang)
    pad = jnp.zeros((S, LANES - QK_NOPE - QK_ROPE), F32)
    cos_tab = jnp.concatenate([jnp.ones((S, QK_NOPE), F32), cos, cos, pad], axis=1)
    sin_tab = jnp.concatenate([jnp.zeros((S, QK_NOPE), F32), sin, sin, pad], axis=1)
    return cos_tab, sin_tab, cos_tab.T, sin_tab.T


def _na_tables(rpb):
    qc = jnp.arange(GRID_W)[:, None]
    kc = jnp.arange(GRID_W)[None, :]
    dc = jnp.clip(kc - qc + (NA_WIN_C - 1), 0, 2 * NA_WIN_C - 2)
    qs = jnp.clip(qc - NA_WIN_C // 2, 0, GRID_W - NA_WIN_C)
    col_valid = (kc >= qs) & (kc < qs + NA_WIN_C)
    rpb = rpb.astype(F32)
    tile = jnp.zeros(rpb.shape[:2] + (GRID_W, GRID_W), F32)
    for e in range(2 * NA_WIN_C - 1):
        tile = jnp.where(dc == e, rpb[:, :, e][:, :, None, None], tile)
    tile = jnp.where(col_valid, tile * LOG2_E, NEG_INF)
    outside = jnp.full((NA_HEADS, GRID_W, GRID_W), NEG_INF, F32)
    variants = []
    for shift, lo in ((0, lambda j: 0), (-NA_WIN_R // 2, lambda j: j), (-NA_WIN_R, lambda j: NA_BROWS - NA_WIN_R)):
        qrows = []
        for j in range(NA_QROWS):
            blocks = [tile[:, i - j + shift + NA_WIN_R - 1] if lo(j) <= i < lo(j) + NA_WIN_R else outside
                      for i in range(NA_BROWS)]
            qrows.append(jnp.concatenate(blocks, axis=2))
        variants.append(jnp.concatenate(qrows, axis=1))
    return jnp.swapaxes(jnp.stack(variants), 2, 3)


def _layer(x, wts, na_tab):
    B, S, _ = x.shape
    tm = 512
    qt, k, vt, nqt, nk, nvt = _proj(x, wts, _rope_tables(S), tm, tk=min(2048, S // 2))
    a = _mla(qt, k, vt)
    n = _natten(nqt, nk, nvt, na_tab, nblk=min(32, S // NA_QTOK))
    x1 = _oproj(a, n, x, wts, tm)
    return _ffn(x1, wts, tm, fc=256)


def kernel(x_prompt, x_sample, g_mix_pre, w_in, g_q_lat, w_q_up, g_kv_lat, w_kv_up, na_rpb, w_o, g_mix_post,
           g_ffn_pre, w_ffn_up, ffn_conv_w, ffn_conv_b, w_ffn_down, g_ffn_post):
    y_prompt, y_sample = x_prompt, x_sample
    for l in range(g_mix_pre.shape[0]):
        wts = _prep_weights(g_mix_pre[l], w_in[l], g_q_lat[l], w_q_up[l], g_kv_lat[l], w_kv_up[l], w_o[l],
                            g_mix_post[l], g_ffn_pre[l], w_ffn_up[l], ffn_conv_w[l], ffn_conv_b[l],
                            w_ffn_down[l], g_ffn_post[l])
        na_tab = _na_tables(na_rpb[l])
        y_prompt = _layer(y_prompt, wts, na_tab)
        y_sample = _layer(y_sample, wts, na_tab)
    return (y_prompt, y_sample)
```

```python
import functools

import jax
import jax.numpy as jnp
from jax import lax
from jax.experimental import pallas as pl
from jax.experimental.pallas import tpu as pltpu

D_MODEL = 1024
GRID_W = 64
MLA_HEADS = 8
Q_LORA = 256
KV_LORA = 128
QK_NOPE = 64
QK_ROPE = 32
V_DIM = 64
ROPE_THETA = 10000.0
NA_HEADS = 8
NA_DIM = 64
NA_WIN_R = 8
NA_WIN_C = 16
NA_WIDTH = NA_HEADS * NA_DIM
MLA_WIDTH = MLA_HEADS * V_DIM
D_FF = 2816
EPS = 1e-6
NEG_INF = -1e30

LANES = 128
V_ROWS = 80
LOG2_E = 1.4426950408889634
Q_SCALE = (QK_NOPE + QK_ROPE) ** -0.5 * LOG2_E
NA_Q_SCALE = NA_DIM ** -0.5 * LOG2_E
HALO = 8
VMEM_LIMIT = 56 * 1024 * 1024

BF16 = jnp.bfloat16
F32 = jnp.float32


def _rms(x, g):
    return x * lax.rsqrt(jnp.mean(x * x, axis=-1, keepdims=True) + EPS) * g


def _dot(a, b):
    return jnp.dot(a, b, preferred_element_type=F32)


def _const_spec(shape):
    return pl.BlockSpec(shape, lambda *_: (0,) * len(shape))


def _proj_kernel(x_ref, g_pre_ref, w_lat_ref, w_nqt_ref, w_nk_ref, w_nvt_ref, g_q_ref, w_qt_ref,
                 g_kv_ref, w_k_ref, w_vt_ref, cos_ref, sin_ref, cost_ref, sint_ref,
                 qt_ref, k_ref, vt_ref, nqt_ref, nk_ref, nvt_ref):
    hn = _rms(x_ref[0], g_pre_ref[...])
    h = hn.astype(BF16)
    tm = h.shape[0]
    cos = cos_ref[...]
    sin = sin_ref[...]
    cost = cost_ref[...]
    sint = sint_ref[...]

    z = _dot(h, w_lat_ref[...])
    c_q = z[:, :Q_LORA]
    c_kv = z[:, Q_LORA:Q_LORA + KV_LORA]
    k_r = z[:, Q_LORA + KV_LORA:Q_LORA + KV_LORA + LANES]
    k_rrot = z[:, Q_LORA + KV_LORA + LANES:]
    k_pe = k_r * cos + k_rrot * sin

    cqn = _rms(c_q, g_q_ref[...])
    cqnt = cqn.T.astype(BF16)
    qmt = _dot(w_qt_ref[...], cqnt)
    ckvn = _rms(c_kv, g_kv_ref[...])
    kn = _dot(ckvn.astype(BF16), w_k_ref[...])
    vvt = _dot(w_vt_ref[...], ckvn.T.astype(BF16))
    ones_row = lax.broadcasted_iota(jnp.int32, (V_ROWS, tm), 0) == V_DIM
    half = QK_ROPE // 2
    for hd in range(MLA_HEADS):
        sl = slice(hd * LANES, (hd + 1) * LANES)
        qh = qmt[sl]
        qrot = jnp.concatenate([qh[:QK_NOPE], -qh[QK_NOPE + half:QK_NOPE + QK_ROPE], qh[QK_NOPE:QK_NOPE + half],
                                qh[QK_NOPE + QK_ROPE:]], axis=0)
        qt_ref[0, hd, 0] = ((qh * cost + qrot * sint) * Q_SCALE).astype(BF16)
        k_ref[0, hd] = (kn[:, sl] + k_pe).astype(BF16)
        vt_ref[0, hd, 0] = jnp.where(ones_row, 1.0, vvt[hd * V_ROWS:(hd + 1) * V_ROWS]).astype(BF16)

    ht = hn.T.astype(BF16)
    nk = _dot(h, w_nk_ref[...])
    nqt = _dot(w_nqt_ref[...], ht) * NA_Q_SCALE
    nvt = _dot(w_nvt_ref[...], ht)
    low_rows = lax.broadcasted_iota(jnp.int32, (LANES, NA_QTOK), 0) < NA_DIM
    na_ones_row = lax.broadcasted_iota(jnp.int32, (V_ROWS, NA_QTOK), 0) == NA_DIM
    for p in range(NA_HEADS // 2):
        nk_ref[0, p] = nk[:, p * LANES:(p + 1) * LANES].astype(BF16)
        for t in range(tm // NA_QTOK):
            cols = slice(t * NA_QTOK, (t + 1) * NA_QTOK)
            pair = nqt[p * LANES:(p + 1) * LANES, cols]
            nqt_ref[0, 2 * p, t] = jnp.where(low_rows, pair, 0.0).astype(BF16)
            nqt_ref[0, 2 * p + 1, t] = jnp.where(low_rows, 0.0, pair).astype(BF16)
            for hd in (2 * p, 2 * p + 1):
                nvt_ref[0, hd, t] = jnp.where(na_ones_row, 1.0, nvt[hd * V_ROWS:(hd + 1) * V_ROWS, cols]).astype(BF16)


def _proj(x, wts, rope, tm, tk):
    B, S, _ = x.shape
    nt = S // tm
    per = tk // tm
    nqt = tm // NA_QTOK
    hp = NA_HEADS // 2
    head_out = lambda n: pl.BlockSpec((1, n, tm, LANES), lambda b, i: (b, 0, i, 0))
    names = ("g_mix_pre", "w_lat", "w_nqt", "w_nk", "w_nvt", "g_q_lat", "w_qt", "g_kv_lat", "w_k", "w_vt")
    consts = [wts[n] for n in names]
    cos_tab, sin_tab, cos_t, sin_t = rope
    return pl.pallas_call(
        _proj_kernel,
        grid=(B, nt),
        in_specs=[pl.BlockSpec((1, tm, D_MODEL), lambda b, i: (b, i, 0))]
        + [_const_spec(c.shape) for c in consts]
        + [pl.BlockSpec((tm, LANES), lambda b, i: (i, 0))] * 2
        + [pl.BlockSpec((LANES, tm), lambda b, i: (0, i))] * 2,
        out_specs=[pl.BlockSpec((1, MLA_HEADS, 1, LANES, tm), lambda b, i: (b, 0, i, 0, 0)),
                   head_out(MLA_HEADS),
                   pl.BlockSpec((1, MLA_HEADS, 1, V_ROWS, tm), lambda b, i: (b, 0, i // per, 0, i % per)),
                   pl.BlockSpec((1, NA_HEADS, nqt, LANES, NA_QTOK), lambda b, i: (b, 0, i, 0, 0)),
                   head_out(hp),
                   pl.BlockSpec((1, NA_HEADS, nqt, V_ROWS, NA_QTOK), lambda b, i: (b, 0, i, 0, 0))],
        out_shape=[jax.ShapeDtypeStruct((B, MLA_HEADS, nt, LANES, tm), BF16),
                   jax.ShapeDtypeStruct((B, MLA_HEADS, S, LANES), BF16),
                   jax.ShapeDtypeStruct((B, MLA_HEADS, S // tk, V_ROWS, tk), BF16),
                   jax.ShapeDtypeStruct((B, NA_HEADS, S // NA_QTOK, LANES, NA_QTOK), BF16),
                   jax.ShapeDtypeStruct((B, hp, S, LANES), BF16),
                   jax.ShapeDtypeStruct((B, NA_HEADS, S // NA_QTOK, V_ROWS, NA_QTOK), BF16)],
        compiler_params=pltpu.CompilerParams(
            dimension_semantics=("parallel", "parallel"), vmem_limit_bytes=VMEM_LIMIT),
        name="proj",
    )(x, *consts, cos_tab, sin_tab, cos_t, sin_t)


MLA_STRIP = 32
MLA_PV_ROWS = 144


def _mla_kernel(qt_ref, k_ref, vt_ref, o_ref, s_ref, p_ref, acc_ref, st_ref):
    nq, tq = qt_ref.shape[2], qt_ref.shape[4]
    nk, tk = vt_ref.shape[2], vt_ref.shape[4]
    assert nk % 2 == 0 and tk % MLA_STRIP == 0
    i = pl.program_id(2)
    qi = jnp.minimum(i, nq - 1)

    def scores(qj, c, hd, slot):
        ks = pl.ds(pl.multiple_of(c * tk, tk), tk)
        st = _dot(k_ref[0, hd, ks, :], qt_ref[0, hd, qj])
        s_ref[slot, hd] = st
        mx = st[0:MLA_STRIP]
        for r in range(MLA_STRIP, tk, MLA_STRIP):
            mx = jnp.maximum(mx, st[r:r + MLA_STRIP])
        return jnp.max(mx, axis=0, keepdims=True)

    def numerators(hd, slot, m, mx):
        m_new = jnp.maximum(m, mx)
        alpha = jnp.exp2(m - m_new)
        for r in range(0, tk, MLA_STRIP):
            st = s_ref[slot, hd, r:r + MLA_STRIP, :]
            p_ref[slot, hd, r:r + MLA_STRIP, :] = jnp.exp2(st - m_new).astype(BF16)
        return m_new, alpha

    def accumulate(c, hd, slot, alpha):
        vt = jnp.concatenate([vt_ref[0, hd, c], jnp.zeros((MLA_PV_ROWS - V_ROWS, tk), BF16)], axis=0)
        acc_ref[hd] = alpha * acc_ref[hd] + _dot(vt, p_ref[slot, hd])[:V_ROWS]

    def chunk(c, cur, carry):
        nxt = 1 - cur
        wraps = c + 1 == nk
        qj_next = jnp.where(wraps, jnp.minimum(i + 1, nq - 1), qi)
        c_next = jnp.where(wraps, 0, c + 1)
        c_prev = jnp.where(c == 0, nk - 1, c - 1)
        out = []
        for hd in range(2):
            m, alpha_prev, mx = carry[hd]
            mx_next = scores(qj_next, c_next, hd, nxt)
            m_new, alpha = numerators(hd, cur, m, mx)
            accumulate(c_prev, hd, nxt, alpha_prev)
            out.append((m_new, alpha, mx_next))
        return tuple(out)

    def step(c, carry):
        carry = lax.cond(c % 2 == 0, lambda: chunk(c, 0, carry), lambda: chunk(c, 1, carry))

        @pl.when((c == 0) & (i > 0))
        def _():
            outs = []
            for hd in range(2):
                acc = acc_ref[hd]
                outs.append(acc[:V_DIM] / acc[V_DIM:V_DIM + 1])
            o_ref[0] = jnp.concatenate(outs, axis=0).T.astype(o_ref.dtype)

        return carry

    @pl.when(i == 0)
    def _():
        for hd in range(2):
            st_ref[hd, 0] = scores(0, 0, hd, 0)
            st_ref[hd, 1] = jnp.ones((1, tq), F32)
            p_ref[1, hd] = jnp.zeros((tk, tq), BF16)
            acc_ref[hd] = jnp.zeros((V_ROWS, tq), F32)

    init = tuple((jnp.full((1, tq), -jnp.inf, F32), st_ref[hd, 1], st_ref[hd, 0]) for hd in range(2))
    fin = lax.fori_loop(0, jnp.where(i == nq, 1, nk), step, init)
    for hd in range(2):
        st_ref[hd, 0] = fin[hd][2]
        st_ref[hd, 1] = fin[hd][1]


def _mla(qt, k, vt):
    B, H, S, _ = k.shape
    nq, tq = qt.shape[2], qt.shape[4]
    nk, tk = vt.shape[2], vt.shape[4]
    resident = 2 * (2 * S * LANES + nk * V_ROWS * tk) * 2
    scratch = 2 * 2 * tk * tq * (4 + 2)
    once = dict(pipeline_mode=pl.Buffered(1)) if 2 * resident + scratch > VMEM_LIMIT * 3 // 4 else {}
    return pl.pallas_call(
        _mla_kernel,
        grid=(B, H // 2, nq + 1),
        in_specs=[pl.BlockSpec((1, 2, nq, LANES, tq), lambda b, p, i: (b, p, 0, 0, 0), **once),
                  pl.BlockSpec((1, 2, S, LANES), lambda b, p, i: (b, p, 0, 0), **once),
                  pl.BlockSpec((1, 2, nk, V_ROWS, tk), lambda b, p, i: (b, p, 0, 0, 0), **once)],
        out_specs=pl.BlockSpec((1, tq, LANES), lambda b, p, i: (b, jnp.maximum(i - 1, 0), p)),
        out_shape=jax.ShapeDtypeStruct((B, S, MLA_WIDTH), BF16),
        scratch_shapes=[pltpu.VMEM((2, 2, tk, tq), F32), pltpu.VMEM((2, 2, tk, tq), BF16),
                        pltpu.VMEM((2, V_ROWS, tq), F32), pltpu.VMEM((2, 2, 1, tq), F32)],
        compiler_params=pltpu.CompilerParams(
            dimension_semantics=("parallel", "parallel", "arbitrary"), vmem_limit_bytes=VMEM_LIMIT),
        name="mla_flash",
    )(qt, k, vt)


NA_QROWS = 4
NA_BROWS = NA_QROWS + NA_WIN_R
NA_QTOK = NA_QROWS * GRID_W
NA_BTOK = NA_BROWS * GRID_W
NA_STRIP = 32
NA_TILES = NA_BROWS // NA_QROWS


def _na_kernel(qt_ref, k_ref, vt_ref, tab_ref, o_ref, s_ref, p_ref, *, nblocks, nblk):
    i = pl.program_id(2)

    def band(j):
        g = i * nblk + j
        t0 = jnp.clip(g - NA_WIN_R // 2 // NA_QROWS, 0, nblocks - NA_TILES)
        var = jnp.where(g == 0, 0, jnp.where(g == nblocks - 1, 2, 1))
        return t0, var

    def scores(j, slot):
        t0, var = band(j)
        kb = k_ref[0, 0, pl.ds(pl.multiple_of(t0 * NA_QTOK, NA_QTOK), NA_BTOK), :]
        mxs = []
        for hd in range(2):
            st = _dot(kb, qt_ref[0, hd, j]) + tab_ref[var, hd]
            s_ref[slot, hd] = st
            mx = st[0:NA_STRIP]
            for r in range(NA_STRIP, NA_BTOK, NA_STRIP):
                mx = jnp.maximum(mx, st[r:r + NA_STRIP])
            mxs.append(jnp.max(mx, axis=0, keepdims=True))
        return tuple(mxs)

    def numerators(slot, mxs):
        for hd in range(2):
            for r in range(0, NA_BTOK, NA_STRIP):
                st = s_ref[slot, hd, r:r + NA_STRIP, :]
                p_ref[slot, hd, r:r + NA_STRIP, :] = jnp.exp2(st - mxs[hd]).astype(BF16)

    def values(j, slot):
        t0, _ = band(j)
        outs = []
        for hd in range(2):
            acc = _dot(vt_ref[0, hd, t0], p_ref[slot, hd, 0:NA_QTOK, :])
            for t in range(1, NA_TILES):
                acc = acc + _dot(vt_ref[0, hd, t0 + t], p_ref[slot, hd, t * NA_QTOK:(t + 1) * NA_QTOK, :])
            outs.append(acc[:NA_DIM] / acc[NA_DIM:NA_DIM + 1])
        rows = pl.ds(pl.multiple_of(j * NA_QTOK, NA_QTOK), NA_QTOK)
        o_ref[0, rows, :] = jnp.concatenate(outs, axis=0).T.astype(o_ref.dtype)

    def block(j, cur, mxs):
        mxs_next = scores(jnp.minimum(j + 1, nblk - 1), 1 - cur)
        numerators(cur, mxs)
        values(jnp.maximum(j - 1, 0), 1 - cur)
        return mxs_next

    def step(j, mxs):
        return lax.cond(j % 2 == 0, lambda: block(j, 0, mxs), lambda: block(j, 1, mxs))

    mxs0 = scores(0, 0)
    p_ref[1] = jnp.ones(p_ref.shape[1:], BF16)
    lax.fori_loop(0, nblk, step, mxs0)
    values(nblk - 1, (nblk - 1) % 2)


def _natten(qt, k, vt, tab, nblk):
    B, HP, S, _ = k.shape
    nblocks = S // NA_QTOK
    assert nblocks >= NA_TILES and nblocks % nblk == 0 and nblk >= 2
    kern = functools.partial(_na_kernel, nblocks=nblocks, nblk=nblk)
    return pl.pallas_call(
        kern,
        grid=(HP, B, nblocks // nblk),
        in_specs=[pl.BlockSpec((1, 2, nblk, LANES, NA_QTOK), lambda p, b, i: (b, p, i, 0, 0)),
                  pl.BlockSpec((1, 1, S, LANES), lambda p, b, i: (b, p, 0, 0)),
                  pl.BlockSpec((1, 2, nblocks, V_ROWS, NA_QTOK), lambda p, b, i: (b, p, 0, 0, 0)),
                  pl.BlockSpec((3, 2, NA_BTOK, NA_QTOK), lambda p, b, i: (0, p, 0, 0))],
        out_specs=pl.BlockSpec((1, nblk * NA_QTOK, LANES), lambda p, b, i: (b, i, p)),
        out_shape=jax.ShapeDtypeStruct((B, S, NA_WIDTH), BF16),
        scratch_shapes=[pltpu.VMEM((2, 2, NA_BTOK, NA_QTOK), F32), pltpu.VMEM((2, 2, NA_BTOK, NA_QTOK), BF16)],
        compiler_params=pltpu.CompilerParams(
            dimension_semantics=("parallel", "parallel", "arbitrary"), vmem_limit_bytes=VMEM_LIMIT),
        name="natten",
    )(qt, k, vt, tab)


def _oproj_kernel(a_ref, n_ref, x_ref, wa_ref, wn_ref, g_ref, o_ref):
    mix = _dot(a_ref[0], wa_ref[...]) + _dot(n_ref[0], wn_ref[...])
    o_ref[0] = x_ref[0] + _rms(mix, g_ref[...])


def _oproj(a, n, x, wts, tm):
    B, S, _ = x.shape
    tok = lambda w: pl.BlockSpec((1, tm, w), lambda b, i: (b, i, 0))
    consts = [wts["w_o_a"], wts["w_o_n"], wts["g_mix_post"]]
    return pl.pallas_call(
        _oproj_kernel,
        grid=(B, S // tm),
        in_specs=[tok(MLA_WIDTH), tok(NA_WIDTH), tok(D_MODEL)] + [_const_spec(c.shape) for c in consts],
        out_specs=tok(D_MODEL),
        out_shape=jax.ShapeDtypeStruct((B, S, D_MODEL), F32),
        compiler_params=pltpu.CompilerParams(
            dimension_semantics=("parallel", "parallel"), vmem_limit_bytes=VMEM_LIMIT),
        name="out_proj",
    )(a, n, x, *consts)


def _gelu_tanh(x):
    return 0.5 * x * (1.0 + jnp.tanh(0.7978845608028654 * (x + 0.044715 * (x * x * x))))


def _ffn_kernel(x_ref, prev_ref, next_ref, g_pre_ref, w_up_ref, cw_ref, cb_ref, w_dn_ref, g_post_ref, o_ref, act_ref,
                *, fc):
    i = pl.program_id(1)
    last = pl.num_programs(1) - 1
    x = x_ref[0]
    tm = x.shape[0]
    prev = jnp.where(i > 0, prev_ref[0], 0.0)
    nxt = jnp.where(i < last, next_ref[0], 0.0)
    xh = jnp.concatenate([prev, x, nxt], axis=0)
    hn = _rms(xh, g_pre_ref[...]).astype(BF16)
    n_ext = tm + 2 * HALO

    def conv(hx, col):
        w = cw_ref[:, col:col + fc]
        left = pltpu.roll(hx, 1, 0)[HALO:HALO + tm]
        right = pltpu.roll(hx, n_ext - 1, 0)[HALO:HALO + tm]
        return cb_ref[:, col:col + fc] + left * w[0:1] + hx[HALO:HALO + tm] * w[1:2] + right * w[2:3]

    for c in range(D_FF // fc):
        cg = c * fc
        cu = D_FF + c * fc
        g = conv(_dot(hn, w_up_ref[:, cg:cg + fc]), cg)
        u = conv(_dot(hn, w_up_ref[:, cu:cu + fc]), cu)
        act_ref[:, cg:cg + fc] = (_gelu_tanh(g) * u).astype(BF16)
    o_ref[0] = x + _rms(_dot(act_ref[...], w_dn_ref[...]), g_post_ref[...])


def _ffn(x, wts, tm, fc):
    B, S, _ = x.shape
    nb = tm // HALO
    kern = functools.partial(_ffn_kernel, fc=fc)
    consts = [wts["g_ffn_pre"], wts["w_ffn_up"], wts["ffn_conv_w"], wts["ffn_conv_b"], wts["w_ffn_down"],
              wts["g_ffn_post"]]
    return pl.pallas_call(
        kern,
        grid=(B, S // tm),
        in_specs=[pl.BlockSpec((1, tm, D_MODEL), lambda b, i: (b, i, 0)),
                  pl.BlockSpec((1, HALO, D_MODEL), lambda b, i: (b, jnp.maximum(i * nb - 1, 0), 0)),
                  pl.BlockSpec((1, HALO, D_MODEL), lambda b, i: (b, jnp.minimum((i + 1) * nb, S // HALO - 1), 0))]
        + [pl.BlockSpec(c.shape, lambda b, i, nd=c.ndim: (0,) * nd, pipeline_mode=pl.Buffered(1)) for c in consts],
        out_specs=pl.BlockSpec((1, tm, D_MODEL), lambda b, i: (b, i, 0)),
        out_shape=jax.ShapeDtypeStruct((B, S, D_MODEL), F32),
        scratch_shapes=[pltpu.VMEM((tm, D_FF), BF16)],
        compiler_params=pltpu.CompilerParams(
            dimension_semantics=("parallel", "parallel"), vmem_limit_bytes=VMEM_LIMIT),
        name="conv_ffn",
    )(x, x, x, *consts)


def _prep_weights(g_mix_pre, w_in, g_q_lat, w_q_up, g_kv_lat, w_kv_up, w_o, g_mix_post,
                  g_ffn_pre, w_ffn_up, ffn_conv_w, ffn_conv_b, w_ffn_down, g_ffn_post):
    half = QK_ROPE // 2
    o1, o2, o3 = Q_LORA, Q_LORA + KV_LORA, Q_LORA + KV_LORA + QK_ROPE
    row = lambda g: g.reshape(1, -1).astype(F32)

    w_kr = w_in[:, o2:o3]
    zl = jnp.zeros((D_MODEL, QK_NOPE), F32)
    zr = jnp.zeros((D_MODEL, LANES - QK_NOPE - QK_ROPE), F32)
    w_kr_blk = jnp.concatenate([zl, w_kr, zr], axis=1)
    w_krrot_blk = jnp.concatenate([zl, -w_kr[:, half:], w_kr[:, :half], zr], axis=1)
    w_lat = jnp.concatenate([w_in[:, :o2], w_kr_blk, w_krrot_blk], axis=1)

    wq = w_q_up.reshape(Q_LORA, MLA_HEADS, QK_NOPE + QK_ROPE)
    zq = jnp.zeros((Q_LORA, MLA_HEADS, LANES - QK_NOPE - QK_ROPE), F32)
    w_q = jnp.concatenate([wq, zq], axis=2).reshape(Q_LORA, MLA_HEADS * LANES)

    wkv = w_kv_up.reshape(KV_LORA, MLA_HEADS, QK_NOPE + V_DIM)
    w_k = jnp.concatenate([wkv[:, :, :QK_NOPE], jnp.zeros((KV_LORA, MLA_HEADS, LANES - QK_NOPE), F32)],
                          axis=2).reshape(KV_LORA, MLA_HEADS * LANES)
    w_v = jnp.concatenate([wkv[:, :, QK_NOPE:], jnp.zeros((KV_LORA, MLA_HEADS, V_ROWS - V_DIM), F32)],
                          axis=2).reshape(KV_LORA, MLA_HEADS * V_ROWS)

    w_nv = jnp.concatenate([w_in[:, o3 + 2 * NA_WIDTH:].reshape(D_MODEL, NA_HEADS, NA_DIM),
                            jnp.zeros((D_MODEL, NA_HEADS, V_ROWS - NA_DIM), F32)],
                           axis=2).reshape(D_MODEL, NA_HEADS * V_ROWS)

    return {
        "g_mix_pre": row(g_mix_pre), "w_lat": w_lat.astype(BF16),
        "w_nqt": w_in[:, o3:o3 + NA_WIDTH].T.astype(BF16),
        "w_nk": w_in[:, o3 + NA_WIDTH:o3 + 2 * NA_WIDTH].astype(BF16),
        "w_nvt": w_nv.T.astype(BF16),
        "g_q_lat": row(g_q_lat), "w_qt": w_q.T.astype(BF16),
        "g_kv_lat": row(g_kv_lat), "w_k": w_k.astype(BF16), "w_vt": w_v.T.astype(BF16),
        "w_o_a": w_o[:MLA_WIDTH].astype(BF16), "w_o_n": w_o[MLA_WIDTH:].astype(BF16), "g_mix_post": row(g_mix_post),
        "g_ffn_pre": row(g_ffn_pre), "w_ffn_up": w_ffn_up.astype(BF16), "ffn_conv_w": ffn_conv_w.astype(F32),
        "ffn_conv_b": row(ffn_conv_b), "w_ffn_down": w_ffn_down.astype(BF16), "g_ffn_post": row(g_ffn_post),
    }


def _rope_tables(S):
    inv = 1.0 / (ROPE_THETA ** (jnp.arange(0, QK_ROPE, 2, dtype=F32) / QK_ROPE))
    ang = jnp.arange(S, dtype=F32)[:, None] * inv[None, :]
    cos, sin = jnp.cos(ang), jnp.sin(ang)
    pad = jnp.zeros((S, LANES - QK_NOPE - QK_ROPE), F32)
    cos_tab = jnp.concatenate([jnp.ones((S, QK_NOPE), F32), cos, cos, pad], axis=1)
    sin_tab = jnp.concatenate([jnp.zeros((S, QK_NOPE), F32), sin, sin, pad], axis=1)
    return cos_tab, sin_tab, cos_tab.T, sin_tab.T


def _na_tables(rpb):
    qc = jnp.arange(GRID_W)[:, None]
    kc = jnp.arange(GRID_W)[None, :]
    dc = jnp.clip(kc - qc + (NA_WIN_C - 1), 0, 2 * NA_WIN_C - 2)
    qs = jnp.clip(qc - NA_WIN_C // 2, 0, GRID_W - NA_WIN_C)
    col_valid = (kc >= qs) & (kc < qs + NA_WIN_C)
    rpb = rpb.astype(F32)
    tile = jnp.zeros(rpb.shape[:2] + (GRID_W, GRID_W), F32)
    for e in range(2 * NA_WIN_C - 1):
        tile = jnp.where(dc == e, rpb[:, :, e][:, :, None, None], tile)
    tile = jnp.where(col_valid, tile * LOG2_E, NEG_INF)
    outside = jnp.full((NA_HEADS, GRID_W, GRID_W), NEG_INF, F32)
    variants = []
    for shift, lo in ((0, lambda j: 0), (-NA_WIN_R // 2, lambda j: j), (-NA_WIN_R, lambda j: NA_BROWS - NA_WIN_R)):
        qrows = []
        for j in range(NA_QROWS):
            blocks = [tile[:, i - j + shift + NA_WIN_R - 1] if lo(j) <= i < lo(j) + NA_WIN_R else outside
                      for i in range(NA_BROWS)]
            qrows.append(jnp.concatenate(blocks, axis=2))
        variants.append(jnp.concatenate(qrows, axis=1))
    return jnp.swapaxes(jnp.stack(variants), 2, 3)


def _layer(x, wts, na_tab):
    B, S, _ = x.shape
    tm = 512
    qt, k, vt, nqt, nk, nvt = _proj(x, wts, _rope_tables(S), tm, tk=min(2048, S // 2))
    a = _mla(qt, k, vt)
    n = _natten(nqt, nk, nvt, na_tab, nblk=min(32, S // NA_QTOK))
    x1 = _oproj(a, n, x, wts, tm)
    return _ffn(x1, wts, tm, fc=256)


def kernel(x_prompt, x_sample, g_mix_pre, w_in, g_q_lat, w_q_up, g_kv_lat, w_kv_up, na_rpb, w_o, g_mix_post,
           g_ffn_pre, w_ffn_up, ffn_conv_w, ffn_conv_b, w_ffn_down, g_ffn_post):
    y_prompt, y_sample = x_prompt, x_sample
    for l in range(g_mix_pre.shape[0]):
        wts = _prep_weights(g_mix_pre[l], w_in[l], g_q_lat[l], w_q_up[l], g_kv_lat[l], w_kv_up[l], w_o[l],
                            g_mix_post[l], g_ffn_pre[l], w_ffn_up[l], ffn_conv_w[l], ffn_conv_b[l],
                            w_ffn_down[l], g_ffn_post[l])
        na_tab = _na_tables(na_rpb[l])
        y_prompt = _layer(y_prompt, wts, na_tab)
        y_sample = _layer(y_sample, wts, na_tab)
    return (y_prompt, y_sample)
```

```python
import functools

import jax
import jax.numpy as jnp
from jax import lax
from jax.experimental import pallas as pl
from jax.experimental.pallas import tpu as pltpu

D_MODEL = 1024
GRID_W = 64
MLA_HEADS = 8
Q_LORA = 256
KV_LORA = 128
QK_NOPE = 64
QK_ROPE = 32
V_DIM = 64
ROPE_THETA = 10000.0
NA_HEADS = 8
NA_DIM = 64
NA_WIN_R = 8
NA_WIN_C = 16
NA_WIDTH = NA_HEADS * NA_DIM
MLA_WIDTH = MLA_HEADS * V_DIM
D_FF = 2816
EPS = 1e-6
NEG_INF = -1e30

LANES = 128
V_ROWS = 80
LOG2_E = 1.4426950408889634
Q_SCALE = (QK_NOPE + QK_ROPE) ** -0.5 * LOG2_E
NA_Q_SCALE = NA_DIM ** -0.5 * LOG2_E
HALO = 8
VMEM_LIMIT = 56 * 1024 * 1024

BF16 = jnp.bfloat16
F32 = jnp.float32


def _rms(x, g):
    return x * lax.rsqrt(jnp.mean(x * x, axis=-1, keepdims=True) + EPS) * g


def _dot(a, b):
    return jnp.dot(a, b, preferred_element_type=F32)


def _const_spec(shape):
    return pl.BlockSpec(shape, lambda *_: (0,) * len(shape))


def _proj_kernel(x_ref, g_pre_ref, w_lat_ref, w_nqt_ref, w_nk_ref, w_nvt_ref, g_q_ref, w_qt_ref,
                 g_kv_ref, w_k_ref, w_vt_ref, cos_ref, sin_ref, cost_ref, sint_ref,
                 qt_ref, k_ref, vt_ref, nqt_ref, nk_ref, nvt_ref):
    hn = _rms(x_ref[0], g_pre_ref[...])
    h = hn.astype(BF16)
    tm = h.shape[0]
    cos = cos_ref[...]
    sin = sin_ref[...]
    cost = cost_ref[...]
    sint = sint_ref[...]

    z = _dot(h, w_lat_ref[...])
    c_q = z[:, :Q_LORA]
    c_kv = z[:, Q_LORA:Q_LORA + KV_LORA]
    k_r = z[:, Q_LORA + KV_LORA:Q_LORA + KV_LORA + LANES]
    k_rrot = z[:, Q_LORA + KV_LORA + LANES:]
    k_pe = k_r * cos + k_rrot * sin

    cqn = _rms(c_q, g_q_ref[...])
    cqnt = cqn.T.astype(BF16)
    qmt = _dot(w_qt_ref[...], cqnt)
    ckvn = _rms(c_kv, g_kv_ref[...])
    kn = _dot(ckvn.astype(BF16), w_k_ref[...])
    vvt = _dot(w_vt_ref[...], ckvn.T.astype(BF16))
    ones_row = lax.broadcasted_iota(jnp.int32, (V_ROWS, tm), 0) == V_DIM
    half = QK_ROPE // 2
    for hd in range(MLA_HEADS):
        sl = slice(hd * LANES, (hd + 1) * LANES)
        qh = qmt[sl]
        qrot = jnp.concatenate([qh[:QK_NOPE], -qh[QK_NOPE + half:QK_NOPE + QK_ROPE], qh[QK_NOPE:QK_NOPE + half],
                                qh[QK_NOPE + QK_ROPE:]], axis=0)
        qt_ref[0, hd, 0] = ((qh * cost + qrot * sint) * Q_SCALE).astype(BF16)
        k_ref[0, hd] = (kn[:, sl] + k_pe).astype(BF16)
        vt_ref[0, hd, 0] = jnp.where(ones_row, 1.0, vvt[hd * V_ROWS:(hd + 1) * V_ROWS]).astype(BF16)

    ht = hn.T.astype(BF16)
    nk = _dot(h, w_nk_ref[...])
    nqt = _dot(w_nqt_ref[...], ht) * NA_Q_SCALE
    nvt = _dot(w_nvt_ref[...], ht)
    low_rows = lax.broadcasted_iota(jnp.int32, (LANES, NA_QTOK), 0) < NA_DIM
    na_ones_row = lax.broadcasted_iota(jnp.int32, (V_ROWS, NA_QTOK), 0) == NA_DIM
    for p in range(NA_HEADS // 2):
        nk_ref[0, p] = nk[:, p * LANES:(p + 1) * LANES].astype(BF16)
        for t in range(tm // NA_QTOK):
            cols = slice(t * NA_QTOK, (t + 1) * NA_QTOK)
            pair = nqt[p * LANES:(p + 1) * LANES, cols]
            nqt_ref[0, 2 * p, t] = jnp.where(low_rows, pair, 0.0).astype(BF16)
            nqt_ref[0, 2 * p + 1, t] = jnp.where(low_rows, 0.0, pair).astype(BF16)
            for hd in (2 * p, 2 * p + 1):
                nvt_ref[0, hd, t] = jnp.where(na_ones_row, 1.0, nvt[hd * V_ROWS:(hd + 1) * V_ROWS, cols]).astype(BF16)


def _proj(x, wts, rope, tm, tk):
    B, S, _ = x.shape
    nt = S // tm
    per = tk // tm
    nqt = tm // NA_QTOK
    hp = NA_HEADS // 2
    head_out = lambda n: pl.BlockSpec((1, n, tm, LANES), lambda b, i: (b, 0, i, 0))
    names = ("g_mix_pre", "w_lat", "w_nqt", "w_nk", "w_nvt", "g_q_lat", "w_qt", "g_kv_lat", "w_k", "w_vt")
    consts = [wts[n] for n in names]
    cos_tab, sin_tab, cos_t, sin_t = rope
    return pl.pallas_call(
        _proj_kernel,
        grid=(B, nt),
        in_specs=[pl.BlockSpec((1, tm, D_MODEL), lambda b, i: (b, i, 0))]
        + [_const_spec(c.shape) for c in consts]
        + [pl.BlockSpec((tm, LANES), lambda b, i: (i, 0))] * 2
        + [pl.BlockSpec((LANES, tm), lambda b, i: (0, i))] * 2,
        out_specs=[pl.BlockSpec((1, MLA_HEADS, 1, LANES, tm), lambda b, i: (b, 0, i, 0, 0)),
                   head_out(MLA_HEADS),
                   pl.BlockSpec((1, MLA_HEADS, 1, V_ROWS, tm), lambda b, i: (b, 0, i // per, 0, i % per)),
                   pl.BlockSpec((1, NA_HEADS, nqt, LANES, NA_QTOK), lambda b, i: (b, 0, i, 0, 0)),
                   head_out(hp),
                   pl.BlockSpec((1, NA_HEADS, nqt, V_ROWS, NA_QTOK), lambda b, i: (b, 0, i, 0, 0))],
        out_shape=[jax.ShapeDtypeStruct((B, MLA_HEADS, nt, LANES, tm), BF16),
                   jax.ShapeDtypeStruct((B, MLA_HEADS, S, LANES), BF16),
                   jax.ShapeDtypeStruct((B, MLA_HEADS, S // tk, V_ROWS, tk), BF16),
                   jax.ShapeDtypeStruct((B, NA_HEADS, S // NA_QTOK, LANES, NA_QTOK), BF16),
                   jax.ShapeDtypeStruct((B, hp, S, LANES), BF16),
                   jax.ShapeDtypeStruct((B, NA_HEADS, S // NA_QTOK, V_ROWS, NA_QTOK), BF16)],
        compiler_params=pltpu.CompilerParams(
            dimension_semantics=("parallel", "parallel"), vmem_limit_bytes=VMEM_LIMIT),
        name="proj",
    )(x, *consts, cos_tab, sin_tab, cos_t, sin_t)


MLA_STRIP = 32
MLA_PV_ROWS = 112


def _mla_kernel(qt_ref, k_ref, vt_ref, o_ref, s_ref, p_ref, acc_ref, st_ref):
    nq, tq = qt_ref.shape[2], qt_ref.shape[4]
    nk, tk = vt_ref.shape[2], vt_ref.shape[4]
    assert nk % 2 == 0 and tk % MLA_STRIP == 0
    i = pl.program_id(2)
    qi = jnp.minimum(i, nq - 1)

    def scores(qj, c, hd, slot):
        ks = pl.ds(pl.multiple_of(c * tk, tk), tk)
        st = _dot(k_ref[0, hd, ks, :], qt_ref[0, hd, qj])
        s_ref[slot, hd] = st
        mx = st[0:MLA_STRIP]
        for r in range(MLA_STRIP, tk, MLA_STRIP):
            mx = jnp.maximum(mx, st[r:r + MLA_STRIP])
        return jnp.max(mx, axis=0, keepdims=True)

    def numerators(hd, slot, m, mx):
        m_new = jnp.maximum(m, mx)
        alpha = jnp.exp2(m - m_new)
        for r in range(0, tk, MLA_STRIP):
            st = s_ref[slot, hd, r:r + MLA_STRIP, :]
            p_ref[slot, hd, r:r + MLA_STRIP, :] = jnp.exp2(st - m_new).astype(BF16)
        return m_new, alpha

    def accumulate(c, hd, slot, alpha):
        vt = jnp.concatenate([vt_ref[0, hd, c], jnp.zeros((MLA_PV_ROWS - V_ROWS, tk), BF16)], axis=0)
        acc_ref[hd] = alpha * acc_ref[hd] + _dot(vt, p_ref[slot, hd])[:V_ROWS]

    def chunk(c, cur, carry):
        nxt = 1 - cur
        wraps = c + 1 == nk
        qj_next = jnp.where(wraps, jnp.minimum(i + 1, nq - 1), qi)
        c_next = jnp.where(wraps, 0, c + 1)
        c_prev = jnp.where(c == 0, nk - 1, c - 1)
        out = []
        for hd in range(2):
            m, alpha_prev, mx = carry[hd]
            mx_next = scores(qj_next, c_next, hd, nxt)
            m_new, alpha = numerators(hd, cur, m, mx)
            accumulate(c_prev, hd, nxt, alpha_prev)
            out.append((m_new, alpha, mx_next))
        return tuple(out)

    def step(c, carry):
        carry = lax.cond(c % 2 == 0, lambda: chunk(c, 0, carry), lambda: chunk(c, 1, carry))

        @pl.when((c == 0) & (i > 0))
        def _():
            outs = []
            for hd in range(2):
                acc = acc_ref[hd]
                outs.append(acc[:V_DIM] / acc[V_DIM:V_DIM + 1])
            o_ref[0] = jnp.concatenate(outs, axis=0).T.astype(o_ref.dtype)

        return carry

    @pl.when(i == 0)
    def _():
        for hd in range(2):
            st_ref[hd, 0] = scores(0, 0, hd, 0)
            st_ref[hd, 1] = jnp.ones((1, tq), F32)
            p_ref[1, hd] = jnp.zeros((tk, tq), BF16)
            acc_ref[hd] = jnp.zeros((V_ROWS, tq), F32)

    init = tuple((jnp.full((1, tq), -jnp.inf, F32), st_ref[hd, 1], st_ref[hd, 0]) for hd in range(2))
    fin = lax.fori_loop(0, jnp.where(i == nq, 1, nk), step, init)
    for hd in range(2):
        st_ref[hd, 0] = fin[hd][2]
        st_ref[hd, 1] = fin[hd][1]


def _mla(qt, k, vt):
    B, H, S, _ = k.shape
    nq, tq = qt.shape[2], qt.shape[4]
    nk, tk = vt.shape[2], vt.shape[4]
    resident = 2 * (2 * S * LANES + nk * V_ROWS * tk) * 2
    scratch = 2 * 2 * tk * tq * (4 + 2)
    once = dict(pipeline_mode=pl.Buffered(1)) if 2 * resident + scratch > VMEM_LIMIT * 3 // 4 else {}
    return pl.pallas_call(
        _mla_kernel,
        grid=(B, H // 2, nq + 1),
        in_specs=[pl.BlockSpec((1, 2, nq, LANES, tq), lambda b, p, i: (b, p, 0, 0, 0), **once),
                  pl.BlockSpec((1, 2, S, LANES), lambda b, p, i: (b, p, 0, 0), **once),
                  pl.BlockSpec((1, 2, nk, V_ROWS, tk), lambda b, p, i: (b, p, 0, 0, 0), **once)],
        out_specs=pl.BlockSpec((1, tq, LANES), lambda b, p, i: (b, jnp.maximum(i - 1, 0), p)),
        out_shape=jax.ShapeDtypeStruct((B, S, MLA_WIDTH), BF16),
        scratch_shapes=[pltpu.VMEM((2, 2, tk, tq), F32), pltpu.VMEM((2, 2, tk, tq), BF16),
                        pltpu.VMEM((2, V_ROWS, tq), F32), pltpu.VMEM((2, 2, 1, tq), F32)],
        compiler_params=pltpu.CompilerParams(
            dimension_semantics=("parallel", "parallel", "arbitrary"), vmem_limit_bytes=VMEM_LIMIT),
        name="mla_flash",
    )(qt, k, vt)


NA_QROWS = 4
NA_BROWS = NA_QROWS + NA_WIN_R
NA_QTOK = NA_QROWS * GRID_W
NA_BTOK = NA_BROWS * GRID_W
NA_STRIP = 32
NA_TILES = NA_BROWS // NA_QROWS


def _na_kernel(qt_ref, k_ref, vt_ref, tab_ref, o_ref, s_ref, p_ref, *, nblocks, nblk):
    i = pl.program_id(2)

    def band(j):
        g = i * nblk + j
        t0 = jnp.clip(g - NA_WIN_R // 2 // NA_QROWS, 0, nblocks - NA_TILES)
        var = jnp.where(g == 0, 0, jnp.where(g == nblocks - 1, 2, 1))
        return t0, var

    def scores(j, slot):
        t0, var = band(j)
        kb = k_ref[0, 0, pl.ds(pl.multiple_of(t0 * NA_QTOK, NA_QTOK), NA_BTOK), :]
        mxs = []
        for hd in range(2):
            st = _dot(kb, qt_ref[0, hd, j]) + tab_ref[var, hd]
            s_ref[slot, hd] = st
            mx = st[0:NA_STRIP]
            for r in range(NA_STRIP, NA_BTOK, NA_STRIP):
                mx = jnp.maximum(mx, st[r:r + NA_STRIP])
            mxs.append(jnp.max(mx, axis=0, keepdims=True))
        return tuple(mxs)

    def numerators(slot, mxs):
        for hd in range(2):
            for r in range(0, NA_BTOK, NA_STRIP):
                st = s_ref[slot, hd, r:r + NA_STRIP, :]
                p_ref[slot, hd, r:r + NA_STRIP, :] = jnp.exp2(st - mxs[hd]).astype(BF16)

    def values(j, slot):
        t0, _ = band(j)
        outs = []
        for hd in range(2):
            acc = _dot(vt_ref[0, hd, t0], p_ref[slot, hd, 0:NA_QTOK, :])
            for t in range(1, NA_TILES):
                acc = acc + _dot(vt_ref[0, hd, t0 + t], p_ref[slot, hd, t * NA_QTOK:(t + 1) * NA_QTOK, :])
            outs.append(acc[:NA_DIM] / acc[NA_DIM:NA_DIM + 1])
        rows = pl.ds(pl.multiple_of(j * NA_QTOK, NA_QTOK), NA_QTOK)
        o_ref[0, rows, :] = jnp.concatenate(outs, axis=0).T.astype(o_ref.dtype)

    def block(j, cur, mxs):
        mxs_next = scores(jnp.minimum(j + 1, nblk - 1), 1 - cur)
        numerators(cur, mxs)
        values(jnp.maximum(j - 1, 0), 1 - cur)
        return mxs_next

    def step(j, mxs):
        return lax.cond(j % 2 == 0, lambda: block(j, 0, mxs), lambda: block(j, 1, mxs))

    mxs0 = scores(0, 0)
    p_ref[1] = jnp.ones(p_ref.shape[1:], BF16)
    lax.fori_loop(0, nblk, step, mxs0)
    values(nblk - 1, (nblk - 1) % 2)


def _natten(qt, k, vt, tab, nblk):
    B, HP, S, _ = k.shape
    nblocks = S // NA_QTOK
    assert nblocks >= NA_TILES and nblocks % nblk == 0 and nblk >= 2
    kern = functools.partial(_na_kernel, nblocks=nblocks, nblk=nblk)
    return pl.pallas_call(
        kern,
        grid=(HP, B, nblocks // nblk),
        in_specs=[pl.BlockSpec((1, 2, nblk, LANES, NA_QTOK), lambda p, b, i: (b, p, i, 0, 0)),
                  pl.BlockSpec((1, 1, S, LANES), lambda p, b, i: (b, p, 0, 0)),
                  pl.BlockSpec((1, 2, nblocks, V_ROWS, NA_QTOK), lambda p, b, i: (b, p, 0, 0, 0)),
                  pl.BlockSpec((3, 2, NA_BTOK, NA_QTOK), lambda p, b, i: (0, p, 0, 0))],
        out_specs=pl.BlockSpec((1, nblk * NA_QTOK, LANES), lambda p, b, i: (b, i, p)),
        out_shape=jax.ShapeDtypeStruct((B, S, NA_WIDTH), BF16),
        scratch_shapes=[pltpu.VMEM((2, 2, NA_BTOK, NA_QTOK), F32), pltpu.VMEM((2, 2, NA_BTOK, NA_QTOK), BF16)],
        compiler_params=pltpu.CompilerParams(
            dimension_semantics=("parallel", "parallel", "arbitrary"), vmem_limit_bytes=VMEM_LIMIT),
        name="natten",
    )(qt, k, vt, tab)


def _oproj_kernel(a_ref, n_ref, x_ref, wa_ref, wn_ref, g_ref, o_ref):
    mix = _dot(a_ref[0], wa_ref[...]) + _dot(n_ref[0], wn_ref[...])
    o_ref[0] = x_ref[0] + _rms(mix, g_ref[...])


def _oproj(a, n, x, wts, tm):
    B, S, _ = x.shape
    tok = lambda w: pl.BlockSpec((1, tm, w), lambda b, i: (b, i, 0))
    consts = [wts["w_o_a"], wts["w_o_n"], wts["g_mix_post"]]
    return pl.pallas_call(
        _oproj_kernel,
        grid=(B, S // tm),
        in_specs=[tok(MLA_WIDTH), tok(NA_WIDTH), tok(D_MODEL)] + [_const_spec(c.shape) for c in consts],
        out_specs=tok(D_MODEL),
        out_shape=jax.ShapeDtypeStruct((B, S, D_MODEL), F32),
        compiler_params=pltpu.CompilerParams(
            dimension_semantics=("parallel", "parallel"), vmem_limit_bytes=VMEM_LIMIT),
        name="out_proj",
    )(a, n, x, *consts)


def _gelu_tanh(x):
    return 0.5 * x * (1.0 + jnp.tanh(0.7978845608028654 * (x + 0.044715 * (x * x * x))))


def _ffn_kernel(x_ref, prev_ref, next_ref, g_pre_ref, w_up_ref, cw_ref, cb_ref, w_dn_ref, g_post_ref, o_ref, act_ref,
                *, fc):
    i = pl.program_id(1)
    last = pl.num_programs(1) - 1
    x = x_ref[0]
    tm = x.shape[0]
    prev = jnp.where(i > 0, prev_ref[0], 0.0)
    nxt = jnp.where(i < last, next_ref[0], 0.0)
    xh = jnp.concatenate([prev, x, nxt], axis=0)
    hn = _rms(xh, g_pre_ref[...]).astype(BF16)
    n_ext = tm + 2 * HALO

    def conv(hx, col):
        w = cw_ref[:, col:col + fc]
        left = pltpu.roll(hx, 1, 0)[HALO:HALO + tm]
        right = pltpu.roll(hx, n_ext - 1, 0)[HALO:HALO + tm]
        return cb_ref[:, col:col + fc] + left * w[0:1] + hx[HALO:HALO + tm] * w[1:2] + right * w[2:3]

    for c in range(D_FF // fc):
        cg = c * fc
        cu = D_FF + c * fc
        g = conv(_dot(hn, w_up_ref[:, cg:cg + fc]), cg)
        u = conv(_dot(hn, w_up_ref[:, cu:cu + fc]), cu)
        act_ref[:, cg:cg + fc] = (_gelu_tanh(g) * u).astype(BF16)
    o_ref[0] = x + _rms(_dot(act_ref[...], w_dn_ref[...]), g_post_ref[...])


def _ffn(x, wts, tm, fc):
    B, S, _ = x.shape
    nb = tm // HALO
    kern = functools.partial(_ffn_kernel, fc=fc)
    consts = [wts["g_ffn_pre"], wts["w_ffn_up"], wts["ffn_conv_w"], wts["ffn_conv_b"], wts["w_ffn_down"],
              wts["g_ffn_post"]]
    return pl.pallas_call(
        kern,
        grid=(B, S // tm),
        in_specs=[pl.BlockSpec((1, tm, D_MODEL), lambda b, i: (b, i, 0)),
                  pl.BlockSpec((1, HALO, D_MODEL), lambda b, i: (b, jnp.maximum(i * nb - 1, 0), 0)),
                  pl.BlockSpec((1, HALO, D_MODEL), lambda b, i: (b, jnp.minimum((i + 1) * nb, S // HALO - 1), 0))]
        + [pl.BlockSpec(c.shape, lambda b, i, nd=c.ndim: (0,) * nd, pipeline_mode=pl.Buffered(1)) for c in consts],
        out_specs=pl.BlockSpec((1, tm, D_MODEL), lambda b, i: (b, i, 0)),
        out_shape=jax.ShapeDtypeStruct((B, S, D_MODEL), F32),
        scratch_shapes=[pltpu.VMEM((tm, D_FF), BF16)],
        compiler_params=pltpu.CompilerParams(
            dimension_semantics=("parallel", "parallel"), vmem_limit_bytes=VMEM_LIMIT),
        name="conv_ffn",
    )(x, x, x, *consts)


def _prep_weights(g_mix_pre, w_in, g_q_lat, w_q_up, g_kv_lat, w_kv_up, w_o, g_mix_post,
                  g_ffn_pre, w_ffn_up, ffn_conv_w, ffn_conv_b, w_ffn_down, g_ffn_post):
    half = QK_ROPE // 2
    o1, o2, o3 = Q_LORA, Q_LORA + KV_LORA, Q_LORA + KV_LORA + QK_ROPE
    row = lambda g: g.reshape(1, -1).astype(F32)

    w_kr = w_in[:, o2:o3]
    zl = jnp.zeros((D_MODEL, QK_NOPE), F32)
    zr = jnp.zeros((D_MODEL, LANES - QK_NOPE - QK_ROPE), F32)
    w_kr_blk = jnp.concatenate([zl, w_kr, zr], axis=1)
    w_krrot_blk = jnp.concatenate([zl, -w_kr[:, half:], w_kr[:, :half], zr], axis=1)
    w_lat = jnp.concatenate([w_in[:, :o2], w_kr_blk, w_krrot_blk], axis=1)

    wq = w_q_up.reshape(Q_LORA, MLA_HEADS, QK_NOPE + QK_ROPE)
    zq = jnp.zeros((Q_LORA, MLA_HEADS, LANES - QK_NOPE - QK_ROPE), F32)
    w_q = jnp.concatenate([wq, zq], axis=2).reshape(Q_LORA, MLA_HEADS * LANES)

    wkv = w_kv_up.reshape(KV_LORA, MLA_HEADS, QK_NOPE + V_DIM)
    w_k = jnp.concatenate([wkv[:, :, :QK_NOPE], jnp.zeros((KV_LORA, MLA_HEADS, LANES - QK_NOPE), F32)],
                          axis=2).reshape(KV_LORA, MLA_HEADS * LANES)
    w_v = jnp.concatenate([wkv[:, :, QK_NOPE:], jnp.zeros((KV_LORA, MLA_HEADS, V_ROWS - V_DIM), F32)],
                          axis=2).reshape(KV_LORA, MLA_HEADS * V_ROWS)

    w_nv = jnp.concatenate([w_in[:, o3 + 2 * NA_WIDTH:].reshape(D_MODEL, NA_HEADS, NA_DIM),
                            jnp.zeros((D_MODEL, NA_HEADS, V_ROWS - NA_DIM), F32)],
                           axis=2).reshape(D_MODEL, NA_HEADS * V_ROWS)

    return {
        "g_mix_pre": row(g_mix_pre), "w_lat": w_lat.astype(BF16),
        "w_nqt": w_in[:, o3:o3 + NA_WIDTH].T.astype(BF16),
        "w_nk": w_in[:, o3 + NA_WIDTH:o3 + 2 * NA_WIDTH].astype(BF16),
        "w_nvt": w_nv.T.astype(BF16),
        "g_q_lat": row(g_q_lat), "w_qt": w_q.T.astype(BF16),
        "g_kv_lat": row(g_kv_lat), "w_k": w_k.astype(BF16), "w_vt": w_v.T.astype(BF16),
        "w_o_a": w_o[:MLA_WIDTH].astype(BF16), "w_o_n": w_o[MLA_WIDTH:].astype(BF16), "g_mix_post": row(g_mix_post),
        "g_ffn_pre": row(g_ffn_pre), "w_ffn_up": w_ffn_up.astype(BF16), "ffn_conv_w": ffn_conv_w.astype(F32),
        "ffn_conv_b": row(ffn_conv_b), "w_ffn_down": w_ffn_down.astype(BF16), "g_ffn_post": row(g_ffn_post),
    }


def _rope_tables(S):
    inv = 1.0 / (ROPE_THETA ** (jnp.arange(0, QK_ROPE, 2, dtype=F32) / QK_ROPE))
    ang = jnp.arange(S, dtype=F32)[:, None] * inv[None, :]
    cos, sin = jnp.cos(ang), jnp.sin(ang)
    pad = jnp.zeros((S, LANES - QK_NOPE - QK_ROPE), F32)
    cos_tab = jnp.concatenate([jnp.ones((S, QK_NOPE), F32), cos, cos, pad], axis=1)
    sin_tab = jnp.concatenate([jnp.zeros((S, QK_NOPE), F32), sin, sin, pad], axis=1)
    return cos_tab, sin_tab, cos_tab.T, sin_tab.T


def _na_tables(rpb):
    qc = jnp.arange(GRID_W)[:, None]
    kc = jnp.arange(GRID_W)[None, :]
    dc = jnp.clip(kc - qc + (NA_WIN_C - 1), 0, 2 * NA_WIN_C - 2)
    qs = jnp.clip(qc - NA_WIN_C // 2, 0, GRID_W - NA_WIN_C)
    col_valid = (kc >= qs) & (kc < qs + NA_WIN_C)
    rpb = rpb.astype(F32)
    tile = jnp.zeros(rpb.shape[:2] + (GRID_W, GRID_W), F32)
    for e in range(2 * NA_WIN_C - 1):
        tile = jnp.where(dc == e, rpb[:, :, e][:, :, None, None], tile)
    tile = jnp.where(col_valid, tile * LOG2_E, NEG_INF)
    outside = jnp.full((NA_HEADS, GRID_W, GRID_W), NEG_INF, F32)
    variants = []
    for shift, lo in ((0, lambda j: 0), (-NA_WIN_R // 2, lambda j: j), (-NA_WIN_R, lambda j: NA_BROWS - NA_WIN_R)):
        qrows = []
        for j in range(NA_QROWS):
            blocks = [tile[:, i - j + shift + NA_WIN_R - 1] if lo(j) <= i < lo(j) + NA_WIN_R else outside
                      for i in range(NA_BROWS)]
            qrows.append(jnp.concatenate(blocks, axis=2))
        variants.append(jnp.concatenate(qrows, axis=1))
    return jnp.swapaxes(jnp.stack(variants), 2, 3)


def _layer(x, wts, na_tab):
    B, S, _ = x.shape
    tm = 512
    qt, k, vt, nqt, nk, nvt = _proj(x, wts, _rope_tables(S), tm, tk=min(2048, S // 2))
    a = _mla(qt, k, vt)
    n = _natten(nqt, nk, nvt, na_tab, nblk=min(32, S // NA_QTOK))
    x1 = _oproj(a, n, x, wts, tm)
    return _ffn(x1, wts, tm, fc=256)


def kernel(x_prompt, x_sample, g_mix_pre, w_in, g_q_lat, w_q_up, g_kv_lat, w_kv_up, na_rpb, w_o, g_mix_post,
           g_ffn_pre, w_ffn_up, ffn_conv_w, ffn_conv_b, w_ffn_down, g_ffn_post):
    y_prompt, y_sample = x_prompt, x_sample
    for l in range(g_mix_pre.shape[0]):
        wts = _prep_weights(g_mix_pre[l], w_in[l], g_q_lat[l], w_q_up[l], g_kv_lat[l], w_kv_up[l], w_o[l],
                            g_mix_post[l], g_ffn_pre[l], w_ffn_up[l], ffn_conv_w[l], ffn_conv_b[l],
                            w_ffn_down[l], g_ffn_post[l])
        na_tab = _na_tables(na_rpb[l])
        y_prompt = _layer(y_prompt, wts, na_tab)
        y_sample = _layer(y_sample, wts, na_tab)
    return (y_prompt, y_sample)
```
